```python
import jax, jax.numpy as jnp
from jax import lax
import numpy as np

D_MODEL = 1024
BATCH = 8
SEQ = 2048
DEPTH = 4

CTX_LEN = 256
GRID_W = 64
F_WIDTH = D_MODEL // 2
F_GROUPS = 4
F_GROUP_DIM = F_WIDTH // F_GROUPS
HEAD_DIM = 64
NA_WIDTH = D_MODEL - F_WIDTH
NA_HEADS = NA_WIDTH // HEAD_DIM
MIX_WIDTH = F_WIDTH + NA_WIDTH
IN_COLS = 2 * F_WIDTH + 4 * NA_WIDTH
WIN_ROWS_MAX = 8
WIN_COLS = 16
EPS = 1e-6
NEG_INF = -1e30

kernel_name = "hybrid_fourier_natten_dit_block"


def _rms_norm(t, g):
    t32 = t.astype(jnp.float32)
    y = t32 * lax.rsqrt(jnp.mean(t32 * t32, axis=-1, keepdims=True) + EPS)
    return (y * g.astype(jnp.float32)).astype(t.dtype)


def _ada(cond, w_ada, b_ada):
    mod = jax.nn.silu(cond) @ w_ada + b_ada
    return jnp.split(mod, 3, axis=-1)


def _split_proj(p):
    f_in = p[..., :F_WIDTH]
    f_gate = p[..., F_WIDTH:2 * F_WIDTH]
    q, k, v, na_gate = jnp.split(p[..., 2 * F_WIDTH:], 4, axis=-1)
    return f_in, f_gate, q, k, v, na_gate


def _heads(t):
    return t.reshape(t.shape[0], t.shape[1], NA_HEADS, HEAD_DIM)


def _fourier(u, w_four):
    b, l, _ = u.shape
    ug = u.reshape(b, l, F_GROUPS, F_GROUP_DIM).astype(jnp.float32)
    y = jnp.real(jnp.fft.fft2(ug, axes=(1, 3))) * ((l * F_GROUP_DIM) ** -0.5)
    y = y.reshape(b, l, F_WIDTH).astype(u.dtype)
    return y @ w_four


def _ctx_attention(qc, kc, vc):
    s = jnp.einsum('bqhd,bkhd->bhqk', qc, kc, preferred_element_type=jnp.float32) * (HEAD_DIM ** -0.5)
    p = jax.nn.softmax(s, axis=-1).astype(vc.dtype)
    o = jnp.einsum('bhqk,bkhd->bqhd', p, vc)
    return o.reshape(o.shape[0], o.shape[1], NA_WIDTH)


def _neighbourhood_attention(q, k, v, kc, vc, rel_bias):
    b, l, h, d = q.shape
    rows = l // GRID_W
    wr = min(WIN_ROWS_MAX, rows)
    r_ar = jnp.arange(rows)
    rs = jnp.clip(r_ar - wr // 2, 0, rows - wr)
    row_idx = rs[:, None] + jnp.arange(wr)[None, :]
    col_ar = jnp.arange(GRID_W)
    cs = jnp.clip(col_ar - WIN_COLS // 2, 0, GRID_W - WIN_COLS)
    in_win = (col_ar[None, :] >= cs[:, None]) & (col_ar[None, :] < cs[:, None] + WIN_COLS)
    mask = jnp.broadcast_to(in_win[:, None, :], (GRID_W, wr, GRID_W)).reshape(GRID_W, wr * GRID_W)
    dr_idx = row_idx - r_ar[:, None] + (WIN_ROWS_MAX - 1)
    dc_idx = jnp.clip(col_ar[None, :] - col_ar[:, None], -(WIN_COLS - 1), WIN_COLS - 1) + (WIN_COLS - 1)
    bias = rel_bias[:, dr_idx[:, None, :, None], dc_idx[None, :, None, :]]
    bias = bias.reshape(h, rows, GRID_W, wr * GRID_W).astype(jnp.float32)
    qg = q.reshape(b, rows, GRID_W, h, d)
    k_rows = k.reshape(b, rows, GRID_W, h, d)[:, row_idx].reshape(b, rows, wr * GRID_W, h, d)
    v_rows = v.reshape(b, rows, GRID_W, h, d)[:, row_idx].reshape(b, rows, wr * GRID_W, h, d)
    scale = HEAD_DIM ** -0.5
    s_loc = jnp.einsum('brqhd,brkhd->bhrqk', qg, k_rows, preferred_element_type=jnp.float32) * scale
    s_loc = jnp.where(mask[None, None, None], s_loc + bias[None], NEG_INF)
    s_ctx = jnp.einsum('brqhd,bkhd->bhrqk', qg, kc, preferred_element_type=jnp.float32) * scale
    p = jax.nn.softmax(jnp.concatenate([s_loc, s_ctx], axis=-1), axis=-1).astype(v.dtype)
    n_loc = wr * GRID_W
    o = (jnp.einsum('bhrqk,brkhd->brqhd', p[..., :n_loc], v_rows)
         + jnp.einsum('bhrqk,bkhd->brqhd', p[..., n_loc:], vc))
    return o.reshape(b, l, NA_WIDTH)


def setup_inputs(seed: int = 0) -> dict:
    key = jax.random.key(seed)
    ks = jax.random.split(key, 14)
    f32 = jnp.float32
    n = lambda k, shape: jax.random.normal(k, shape, f32)
    return {
        "x": n(ks[0], (BATCH, SEQ, D_MODEL)),
        "c": n(ks[1], (BATCH, D_MODEL)),
        "ctx": n(ks[2], (BATCH, CTX_LEN, D_MODEL)),
        "c_ctx": n(ks[3], (D_MODEL,)),
        "norm_g": 1.0 + 0.02 * n(ks[4], (DEPTH, D_MODEL)),
        "w_ada": n(ks[5], (DEPTH, D_MODEL, 3 * D_MODEL)) * D_MODEL ** -0.5,
        "b_ada": 0.02 * n(ks[6], (DEPTH, 3 * D_MODEL)),
        "w_in": n(ks[7], (DEPTH, D_MODEL, IN_COLS)) * D_MODEL ** -0.5,
        "w_four": n(ks[8], (DEPTH, F_WIDTH, F_WIDTH)) * F_WIDTH ** -0.5,
        "q_norm_g": 1.0 + 0.02 * n(ks[9], (DEPTH, HEAD_DIM)),
        "k_norm_g": 1.0 + 0.02 * n(ks[10], (DEPTH, HEAD_DIM)),
        "rel_bias": 0.02 * n(ks[11], (DEPTH, NA_HEADS, 2 * WIN_ROWS_MAX - 1, 2 * WIN_COLS - 1)),
        "w_out": n(ks[12], (DEPTH, MIX_WIDTH, D_MODEL)) * MIX_WIDTH ** -0.5,
    }


def reference(x, c, ctx, c_ctx, norm_g, w_ada, b_ada, w_in, w_four, q_norm_g, k_norm_g, rel_bias, w_out):
    for l in range(DEPTH):
        last = l == DEPTH - 1
        sh_x, sc_x, g_x = [t[:, None, :] for t in _ada(c, w_ada[l], b_ada[l])]
        sh_c, sc_c, g_c = _ada(c_ctx, w_ada[l], b_ada[l])
        xn = _rms_norm(x, norm_g[l]) * (1.0 + sc_x) + sh_x
        cn = _rms_norm(ctx, norm_g[l]) * (1.0 + sc_c) + sh_c
        fx_in, fx_gate, qx, kx, vx, nx_gate = _split_proj(xn @ w_in[l])
        fc_in, fc_gate, qc, kc, vc, nc_gate = _split_proj(cn @ w_in[l])
        qx = _rms_norm(_heads(qx), q_norm_g[l])
        kx = _rms_norm(_heads(kx), k_norm_g[l])
        vx = _heads(vx)
        qc = _rms_norm(_heads(qc), q_norm_g[l])
        kc = _rms_norm(_heads(kc), k_norm_g[l])
        vc = _heads(vc)
        f_out = _fourier(fx_in, w_four[l]) * jax.nn.silu(fx_gate)
        na_out = _neighbourhood_attention(qx, kx, vx, kc, vc, rel_bias[l]) * jax.nn.silu(nx_gate)
        x_new = x + g_x * (jnp.concatenate([f_out, na_out], axis=-1) @ w_out[l])
        if not last:
            fc_out = _fourier(fc_in, w_four[l]) * jax.nn.silu(fc_gate)
            nc_out = _ctx_attention(qc, kc, vc) * jax.nn.silu(nc_gate)
            ctx = ctx + g_c * (jnp.concatenate([fc_out, nc_out], axis=-1) @ w_out[l])
        x = x_new
    return x
```

```python
import functools

import numpy as np
import jax
import jax.numpy as jnp
from jax.experimental import pallas as pl
from jax.experimental.pallas import tpu as pltpu

D_MODEL = 1024
DEPTH = 4
GRID_W = 64
F_WIDTH = 512
F_GROUPS = 4
F_GROUP_DIM = F_WIDTH // F_GROUPS
HEAD_DIM = 64
NA_WIDTH = 512
NA_HEADS = NA_WIDTH // HEAD_DIM
IN_COLS = 2 * F_WIDTH + 4 * NA_WIDTH
WIN_ROWS = 8
WIN_COLS = 16
EPS = 1e-6
NEG_INF = -1e30

GROUP_W = 512
N_GROUPS = IN_COLS // GROUP_W
QUAD_W = 256
HEADS_PER_QUAD = QUAD_W // HEAD_DIM
N_QUADS = NA_WIDTH // QUAD_W
N_BIAS_PATTERNS = 8
MOD_ROWS = 16

VMEM_LIMIT = 56 * 1024 * 1024


def _cparams(*sem):
    return pltpu.CompilerParams(dimension_semantics=sem, vmem_limit_bytes=VMEM_LIMIT)


def _dot(a, b):
    return jnp.dot(a, b, preferred_element_type=jnp.float32)


def _dot_nt(a, b):
    return jax.lax.dot_general(a, b, (((1,), (1,)), ((), ())), preferred_element_type=jnp.float32)


def _silu(t):
    return t * (1.0 / (1.0 + jnp.exp(-t)))


def _mod_kernel(cond_ref, w_ref, b_ref, o_ref):
    cond = cond_ref[...]
    o_ref[0] = _dot(_silu(cond), w_ref[0]) + b_ref[0]


def _modulation(cond, w_ada, b_ada):
    tn = 1024
    return pl.pallas_call(
        _mod_kernel,
        grid=(DEPTH, 3 * D_MODEL // tn),
        in_specs=[
            pl.BlockSpec((MOD_ROWS, D_MODEL), lambda l, j: (0, 0)),
            pl.BlockSpec((1, D_MODEL, tn), lambda l, j: (l, 0, j)),
            pl.BlockSpec((1, 1, tn), lambda l, j: (l, 0, j)),
        ],
        out_specs=pl.BlockSpec((1, MOD_ROWS, tn), lambda l, j: (l, 0, j)),
        out_shape=jax.ShapeDtypeStruct((DEPTH, MOD_ROWS, 3 * D_MODEL), jnp.float32),
        compiler_params=_cparams("parallel", "parallel"),
        name="adaln_mod",
    )(cond, w_ada, b_ada.reshape(DEPTH, 1, 3 * D_MODEL))


def _stream_kernel(has_prev, has_next, *refs):
    refs = list(refs)
    x_ref = refs.pop(0)
    if has_prev:
        fo_ref, na_ref, gate_ref, wout_ref = refs[:4]
        refs = refs[4:]
    if has_next:
        mod_ref, ng_ref, win_ref, hm_ref, qg_ref, kg_ref = refs[:6]
        refs = refs[6:]
    if has_prev:
        xo_ref = refs.pop(0)
    if has_next:
        p_ref = refs.pop(0)

    x = x_ref[0]
    if has_prev:
        mix = _dot(fo_ref[0], wout_ref[:F_WIDTH, :]) + _dot(na_ref[0], wout_ref[F_WIDTH:, :])
        x = x + gate_ref[0] * mix
        xo_ref[0] = x
    if has_next:
        ms = jnp.mean(x * x, axis=-1, keepdims=True)
        y = x * jax.lax.rsqrt(ms + EPS) * ng_ref[...]
        shift = mod_ref[0, :, :D_MODEL]
        scale = mod_ref[0, :, D_MODEL:2 * D_MODEL]
        y = (y * (1.0 + scale) + shift).astype(jnp.bfloat16)
        for j in range(N_GROUPS):
            pj = _dot(y, win_ref[:, j * GROUP_W:(j + 1) * GROUP_W])
            if j in (1, 5):
                pj = _silu(pj)
            elif j in (2, 3):
                sq = (pj * pj).astype(jnp.bfloat16)
                ms_h = _dot(sq, hm_ref[...])
                g = qg_ref[...] if j == 2 else kg_ref[...]
                pj = pj * jax.lax.rsqrt(ms_h + EPS) * g
            p_ref[0, :, j * GROUP_W:(j + 1) * GROUP_W] = pj.astype(jnp.bfloat16)


def _stream(x, mod_index, prev=None, nxt=None):
    b, l, d = x.shape
    tm = min(512, l)
    const2 = lambda i, t: (0, 0)
    in_specs = [pl.BlockSpec((1, tm, d), lambda i, t: (i, t, 0))]
    args = [x]
    out_specs, out_shape = [], []
    if prev is not None:
        f_out, na_out, mods, w_out = prev
        in_specs += [
            pl.BlockSpec((1, tm, F_WIDTH), lambda i, t: (i, t, 0)),
            pl.BlockSpec((1, tm, NA_WIDTH), lambda i, t: (i, t, 0)),
            pl.BlockSpec((1, 1, d), lambda i, t: (mod_index(i), 0, 2)),
            pl.BlockSpec((d, d), const2),
        ]
        args += [f_out, na_out, mods, w_out]
        out_specs.append(pl.BlockSpec((1, tm, d), lambda i, t: (i, t, 0)))
        out_shape.append(jax.ShapeDtypeStruct((b, l, d), jnp.float32))
    if nxt is not None:
        mods, ng, w_in, head_mean, qg, kg = nxt
        in_specs += [
            pl.BlockSpec((1, 1, 3 * d), lambda i, t: (mod_index(i), 0, 0)),
            pl.BlockSpec((1, d), const2),
            pl.BlockSpec((d, IN_COLS), const2),
            pl.BlockSpec((GROUP_W, GROUP_W), const2),
            pl.BlockSpec((1, GROUP_W), const2),
            pl.BlockSpec((1, GROUP_W), const2),
        ]
        args += [mods, ng, w_in, head_mean, qg, kg]
        out_specs.append(pl.BlockSpec((1, tm, IN_COLS), lambda i, t: (i, t, 0)))
        out_shape.append(jax.ShapeDtypeStruct((b, l, IN_COLS), jnp.bfloat16))
    outs = pl.pallas_call(
        functools.partial(_stream_kernel, prev is not None, nxt is not None),
        grid=(b, l // tm),
        in_specs=in_specs,
        out_specs=out_specs,
        out_shape=out_shape,
        compiler_params=_cparams("parallel", "parallel"),
        name="stream_" + ("o" if prev is not None else "") + ("i" if nxt is not None else ""),
    )(*args)
    return outs


def _fourier_kernel(seq, tr, u_ref, gate_ref, cs_ref, dl_ref, wf_ref, o_ref, ab_ref):
    t = pl.program_id(1)

    @pl.when(t == 0)
    def _():
        for c in range(seq // tr):
            rows = pl.ds(c * tr, tr)
            uc = _dot(u_ref[0, rows, :], cs_ref[...])
            ab_ref[pl.ds(c * tr, tr), :] = uc[:, :F_WIDTH].astype(jnp.bfloat16)
            ab_ref[pl.ds(seq + c * tr, tr), :] = uc[:, F_WIDTH:].astype(jnp.bfloat16)

    y = _dot(dl_ref[...], ab_ref[...]).astype(jnp.bfloat16)
    o_ref[0] = (_dot(y, wf_ref[...]) * gate_ref[0].astype(jnp.float32)).astype(jnp.bfloat16)


def _fourier(p, chan_tab, seq_tab, w_four):
    b, seq, _ = p.shape
    tr = min(512, seq)
    return pl.pallas_call(
        functools.partial(_fourier_kernel, seq, tr),
        grid=(b, seq // tr),
        in_specs=[
            pl.BlockSpec((1, seq, GROUP_W), lambda i, t: (i, 0, 0)),
            pl.BlockSpec((1, tr, GROUP_W), lambda i, t: (i, t, 1)),
            pl.BlockSpec((F_WIDTH, 2 * F_WIDTH), lambda i, t: (0, 0)),
            pl.BlockSpec((tr, 2 * seq), lambda i, t: (t, 0)),
            pl.BlockSpec((F_WIDTH, F_WIDTH), lambda i, t: (0, 0)),
        ],
        out_specs=pl.BlockSpec((1, tr, F_WIDTH), lambda i, t: (i, t, 0)),
        out_shape=jax.ShapeDtypeStruct((b, seq, F_WIDTH), jnp.bfloat16),
        scratch_shapes=[pltpu.VMEM((2 * seq, F_WIDTH), jnp.bfloat16)],
        compiler_params=_cparams("parallel", "arbitrary"),
        name="fourier",
    )(p, p, chan_tab, seq_tab, w_four)


def _dft_tables(seq):
    n = np.arange(seq, dtype=np.int64)
    ang = 2.0 * np.pi * ((n[:, None] * n[None, :]) % seq) / seq
    seq_tab = np.concatenate([np.cos(ang), -np.sin(ang)], axis=1)
    m = np.arange(F_GROUP_DIM, dtype=np.int64)
    angc = 2.0 * np.pi * ((m[:, None] * m[None, :]) % F_GROUP_DIM) / F_GROUP_DIM
    eye = np.eye(F_GROUPS)
    scale = (seq * F_GROUP_DIM) ** -0.5
    chan_tab = np.concatenate([np.kron(eye, np.cos(angc)), np.kron(eye, np.sin(angc))], axis=1) * scale
    return (jnp.asarray(chan_tab, jnp.float32).astype(jnp.bfloat16),
            jnp.asarray(seq_tab, jnp.float32).astype(jnp.bfloat16))


def _quad_attention(qq, k_loc, v_loc, k_ctx, v_ctx, bias):
    n = qq.shape[0]
    head_of_lane = jax.lax.broadcasted_iota(jnp.int32, (n, QUAD_W), 1) // HEAD_DIM
    zero = jnp.zeros_like(qq)
    lhs = jnp.concatenate([jnp.where(head_of_lane == h, qq, zero) for h in range(HEADS_PER_QUAD)], axis=0)
    s_ctx = _dot_nt(lhs, k_ctx)
    m = jnp.max(s_ctx, axis=-1, keepdims=True)
    if k_loc is not None:
        s_loc = _dot_nt(lhs, k_loc) + bias
        m = jnp.maximum(m, jnp.max(s_loc, axis=-1, keepdims=True))
        e_loc = jnp.exp(s_loc - m)
    e_ctx = jnp.exp(s_ctx - m)
    denom = jnp.sum(e_ctx, axis=-1, keepdims=True)
    o = _dot(e_ctx.astype(jnp.bfloat16), v_ctx)
    if k_loc is not None:
        denom = denom + jnp.sum(e_loc, axis=-1, keepdims=True)
        o = o + _dot(e_loc.astype(jnp.bfloat16), v_loc)
    o = o * (1.0 / denom)
    out = o[:n]
    for h in range(1, HEADS_PER_QUAD):
        out = jnp.where(head_of_lane == h, o[h * n:(h + 1) * n], out)
    return out


def _nattn_kernel(rows_per_step, q_ref, g_ref, k_ref, v_ref, kc_ref, vc_ref, bias_ref, o_ref):
    t = pl.program_id(1)
    n_rows = k_ref.shape[1] // GRID_W

    def row_body(i, carry):
        r = t * rows_per_step + i
        rs = jnp.clip(r - WIN_ROWS // 2, 0, n_rows - WIN_ROWS)
        pat = jnp.minimum(r, WIN_ROWS // 2) + jnp.maximum(r - (n_rows - WIN_ROWS // 2), 0)
        qrows = pl.ds(pl.multiple_of(i * GRID_W, GRID_W), GRID_W)
        band = pl.ds(pl.multiple_of(rs * GRID_W, GRID_W), WIN_ROWS * GRID_W)
        for quad in range(N_QUADS):
            lanes = slice(quad * QUAD_W, (quad + 1) * QUAD_W)
            out = _quad_attention(
                q_ref[0, qrows, lanes],
                k_ref[0, band, lanes], v_ref[0, band, lanes],
                kc_ref[0, :, lanes], vc_ref[0, :, lanes],
                bias_ref[pat, quad].astype(jnp.float32))
            o_ref[0, qrows, lanes] = (out * g_ref[0, qrows, lanes].astype(jnp.float32)).astype(jnp.bfloat16)
        return carry

    jax.lax.fori_loop(0, rows_per_step, row_body, 0)


def _nattn(px, pc, bias):
    b, seq, _ = px.shape
    lc = pc.shape[1]
    rows_per_step = 8
    tq = rows_per_step * GRID_W
    return pl.pallas_call(
        functools.partial(_nattn_kernel, rows_per_step),
        grid=(b, seq // tq),
        in_specs=[
            pl.BlockSpec((1, tq, GROUP_W), lambda i, t: (i, t, 2)),
            pl.BlockSpec((1, tq, GROUP_W), lambda i, t: (i, t, 5)),
            pl.BlockSpec((1, seq, GROUP_W), lambda i, t: (i, 0, 3)),
            pl.BlockSpec((1, seq, GROUP_W), lambda i, t: (i, 0, 4)),
            pl.BlockSpec((1, lc, GROUP_W), lambda i, t: (i, 0, 3)),
            pl.BlockSpec((1, lc, GROUP_W), lambda i, t: (i, 0, 4)),
            pl.BlockSpec((N_BIAS_PATTERNS, N_QUADS, HEADS_PER_QUAD * GRID_W, WIN_ROWS * GRID_W),
                         lambda i, t: (0, 0, 0, 0)),
        ],
        out_specs=pl.BlockSpec((1, tq, NA_WIDTH), lambda i, t: (i, t, 0)),
        out_shape=jax.ShapeDtypeStruct((b, seq, NA_WIDTH), jnp.bfloat16),
        compiler_params=_cparams("parallel", "parallel"),
        name="nattn",
    )(px, px, px, px, pc, pc, bias)


def _cattn_kernel(q_ref, g_ref, k_ref, v_ref, o_ref):
    for quad in range(N_QUADS):
        lanes = slice(quad * QUAD_W, (quad + 1) * QUAD_W)
        out = _quad_attention(q_ref[0, :, lanes], None, None, k_ref[0, :, lanes], v_ref[0, :, lanes], None)
        o_ref[0, :, lanes] = (out * g_ref[0, :, lanes].astype(jnp.float32)).astype(jnp.bfloat16)


def _cattn(pc):
    b, lc, _ = pc.shape
    spec = lambda j: pl.BlockSpec((1, lc, GROUP_W), lambda i: (i, 0, j))
    return pl.pallas_call(
        _cattn_kernel,
        grid=(b,),
        in_specs=[spec(2), spec(5), spec(3), spec(4)],
        out_specs=pl.BlockSpec((1, lc, NA_WIDTH), lambda i: (i, 0, 0)),
        out_shape=jax.ShapeDtypeStruct((b, lc, NA_WIDTH), jnp.bfloat16),
        compiler_params=_cparams("parallel"),
        name="cattn",
    )(pc, pc, pc, pc)


def _bias_tables(rel_bias, n_rows):
    half = WIN_ROWS // 2
    rep_rows = np.array(list(range(half)) + [half] + list(range(n_rows - half + 1, n_rows)))
    rs = np.clip(rep_rows - half, 0, n_rows - WIN_ROWS)
    dr_idx = rs[:, None] + np.arange(WIN_ROWS)[None, :] - rep_rows[:, None] + (WIN_ROWS - 1)
    col = np.arange(GRID_W)
    cs = np.clip(col - WIN_COLS // 2, 0, GRID_W - WIN_COLS)
    in_win = (col[None, :] >= cs[:, None]) & (col[None, :] < cs[:, None] + WIN_COLS)
    dc_idx = np.clip(col[None, :] - col[:, None], -(WIN_COLS - 1), WIN_COLS - 1) + (WIN_COLS - 1)
    tab = rel_bias[:, :, dr_idx[:, None, :, None], dc_idx[None, :, None, :]]
    tab = jnp.where(in_win[None, None, None, :, None, :], tab, NEG_INF)
    tab = jnp.transpose(tab, (0, 2, 1, 3, 4, 5))
    tab = tab.reshape(DEPTH, N_BIAS_PATTERNS, N_QUADS, HEADS_PER_QUAD * GRID_W, WIN_ROWS * GRID_W)
    return tab.astype(jnp.bfloat16)


def kernel(x, c, ctx, c_ctx, norm_g, w_ada, b_ada, w_in, w_four, q_norm_g, k_norm_g, rel_bias, w_out):
    batch, seq, d = x.shape
    bf16 = jnp.bfloat16

    cond = jnp.zeros((MOD_ROWS, d), jnp.float32).at[:batch].set(c).at[batch].set(c_ctx)
    mods = _modulation(cond, w_ada, b_ada)
    mods = mods.reshape(DEPTH, MOD_ROWS, 1, 3 * d)

    w_in_b, w_four_b, w_out_b = w_in.astype(bf16), w_four.astype(bf16), w_out.astype(bf16)
    qg = jnp.tile(q_norm_g, (1, NA_HEADS)) * (HEAD_DIM ** -0.5)
    kg = jnp.tile(k_norm_g, (1, NA_HEADS))
    head_mean = jnp.asarray(np.kron(np.eye(NA_HEADS), np.full((HEAD_DIM, HEAD_DIM), 1.0 / HEAD_DIM)), bf16)
    bias = _bias_tables(rel_bias, seq // GRID_W)
    chan_x, seq_x = _dft_tables(seq)
    chan_c, seq_c = _dft_tables(ctx.shape[1])

    x_mod = lambda i: i
    c_mod = lambda i: batch

    def in_proj_args(l):
        return (mods[l], norm_g[l].reshape(1, d), w_in_b[l], head_mean,
                qg[l].reshape(1, NA_WIDTH), kg[l].reshape(1, NA_WIDTH))

    (px,) = _stream(x, x_mod, nxt=in_proj_args(0))
    (pc,) = _stream(ctx, c_mod, nxt=in_proj_args(0))
    for l in range(DEPTH):
        fo_x = _fourier(px, chan_x, seq_x, w_four_b[l])
        na_x = _nattn(px, pc, bias[l])
        if l + 1 < DEPTH:
            fo_c = _fourier(pc, chan_c, seq_c, w_four_b[l])
            na_c = _cattn(pc)
            x, px = _stream(x, x_mod, prev=(fo_x, na_x, mods[l], w_out_b[l]), nxt=in_proj_args(l + 1))
            ctx, pc = _stream(ctx, c_mod, prev=(fo_c, na_c, mods[l], w_out_b[l]), nxt=in_proj_args(l + 1))
        else:
            (x,) = _stream(x, x_mod, prev=(fo_x, na_x, mods[l], w_out_b[l]))
    return x
```

```python
import functools

import numpy as np
import jax
import jax.numpy as jnp
from jax.experimental import pallas as pl
from jax.experimental.pallas import tpu as pltpu

D_MODEL = 1024
DEPTH = 4
GRID_W = 64
F_WIDTH = 512
F_GROUPS = 4
F_GROUP_DIM = F_WIDTH // F_GROUPS
HEAD_DIM = 64
NA_WIDTH = 512
NA_HEADS = NA_WIDTH // HEAD_DIM
IN_COLS = 2 * F_WIDTH + 4 * NA_WIDTH
WIN_ROWS = 8
WIN_COLS = 16
EPS = 1e-6
NEG_INF = -1e30

GROUP_W = 512
N_GROUPS = IN_COLS // GROUP_W
QUAD_W = 256
HEADS_PER_QUAD = QUAD_W // HEAD_DIM
N_QUADS = NA_WIDTH // QUAD_W
N_BIAS_PATTERNS = 8
MOD_ROWS = 16

VMEM_LIMIT = 56 * 1024 * 1024


def _cparams(*sem):
    return pltpu.CompilerParams(dimension_semantics=sem, vmem_limit_bytes=VMEM_LIMIT)


def _dot(a, b):
    return jnp.dot(a, b, preferred_element_type=jnp.float32)


def _dot_nt(a, b):
    return jax.lax.dot_general(a, b, (((1,), (1,)), ((), ())), preferred_element_type=jnp.float32)


def _silu(t):
    return t * (1.0 / (1.0 + jnp.exp(-t)))


def _mod_kernel(cond_ref, w_ref, b_ref, o_ref):
    cond = cond_ref[...]
    o_ref[0] = _dot(_silu(cond), w_ref[0]) + b_ref[0]


def _modulation(cond, w_ada, b_ada):
    tn = 1024
    return pl.pallas_call(
        _mod_kernel,
        grid=(DEPTH, 3 * D_MODEL // tn),
        in_specs=[
            pl.BlockSpec((MOD_ROWS, D_MODEL), lambda l, j: (0, 0)),
            pl.BlockSpec((1, D_MODEL, tn), lambda l, j: (l, 0, j)),
            pl.BlockSpec((1, 1, tn), lambda l, j: (l, 0, j)),
        ],
        out_specs=pl.BlockSpec((1, MOD_ROWS, tn), lambda l, j: (l, 0, j)),
        out_shape=jax.ShapeDtypeStruct((DEPTH, MOD_ROWS, 3 * D_MODEL), jnp.float32),
        compiler_params=_cparams("parallel", "parallel"),
        name="adaln_mod",
    )(cond, w_ada, b_ada.reshape(DEPTH, 1, 3 * D_MODEL))


def _stream_kernel(has_prev, has_next, *refs):
    refs = list(refs)
    x_ref = refs.pop(0)
    if has_prev:
        fo_ref, na_ref, gate_ref, wout_ref = refs[:4]
        refs = refs[4:]
    if has_next:
        mod_ref, ng_ref, win_ref, hm_ref, qg_ref, kg_ref = refs[:6]
        refs = refs[6:]
    if has_prev:
        xo_ref = refs.pop(0)
    if has_next:
        p_ref = refs.pop(0)

    x = x_ref[0]
    if has_prev:
        mix = _dot(fo_ref[0], wout_ref[:F_WIDTH, :]) + _dot(na_ref[0], wout_ref[F_WIDTH:, :])
        x = x + gate_ref[0] * mix
        xo_ref[0] = x
    if has_next:
        ms = jnp.mean(x * x, axis=-1, keepdims=True)
        y = x * jax.lax.rsqrt(ms + EPS) * ng_ref[...]
        shift = mod_ref[0, :, :D_MODEL]
        scale = mod_ref[0, :, D_MODEL:2 * D_MODEL]
        y = (y * (1.0 + scale) + shift).astype(jnp.bfloat16)
        for j in range(N_GROUPS):
            pj = _dot(y, win_ref[:, j * GROUP_W:(j + 1) * GROUP_W])
            if j in (1, 5):
                pj = _silu(pj)
            elif j in (2, 3):
                sq = (pj * pj).astype(jnp.bfloat16)
                ms_h = _dot(sq, hm_ref[...])
                g = qg_ref[...] if j == 2 else kg_ref[...]
                pj = pj * jax.lax.rsqrt(ms_h + EPS) * g
            p_ref[0, :, j * GROUP_W:(j + 1) * GROUP_W] = pj.astype(jnp.bfloat16)


def _stream(x, mod_index, prev=None, nxt=None):
    b, l, d = x.shape
    tm = min(512, l)
    const2 = lambda i, t: (0, 0)
    in_specs = [pl.BlockSpec((1, tm, d), lambda i, t: (i, t, 0))]
    args = [x]
    out_specs, out_shape = [], []
    if prev is not None:
        f_out, na_out, mods, w_out = prev
        in_specs += [
            pl.BlockSpec((1, tm, F_WIDTH), lambda i, t: (i, t, 0)),
            pl.BlockSpec((1, tm, NA_WIDTH), lambda i, t: (i, t, 0)),
            pl.BlockSpec((1, 1, d), lambda i, t: (mod_index(i), 0, 2)),
            pl.BlockSpec((d, d), const2),
        ]
        args += [f_out, na_out, mods, w_out]
        out_specs.append(pl.BlockSpec((1, tm, d), lambda i, t: (i, t, 0)))
        out_shape.append(jax.ShapeDtypeStruct((b, l, d), jnp.float32))
    if nxt is not None:
        mods, ng, w_in, head_mean, qg, kg = nxt
        in_specs += [
            pl.BlockSpec((1, 1, 3 * d), lambda i, t: (mod_index(i), 0, 0)),
            pl.BlockSpec((1, d), const2),
            pl.BlockSpec((d, IN_COLS), const2),
            pl.BlockSpec((GROUP_W, GROUP_W), const2),
            pl.BlockSpec((1, GROUP_W), const2),
            pl.BlockSpec((1, GROUP_W), const2),
        ]
        args += [mods, ng, w_in, head_mean, qg, kg]
        out_specs.append(pl.BlockSpec((1, tm, IN_COLS), lambda i, t: (i, t, 0)))
        out_shape.append(jax.ShapeDtypeStruct((b, l, IN_COLS), jnp.bfloat16))
    outs = pl.pallas_call(
        functools.partial(_stream_kernel, prev is not None, nxt is not None),
        grid=(b, l // tm),
        in_specs=in_specs,
        out_specs=out_specs,
        out_shape=out_shape,
        compiler_params=_cparams("parallel", "parallel"),
        name="stream_" + ("o" if prev is not None else "") + ("i" if nxt is not None else ""),
    )(*args)
    return outs


def _fourier_kernel(seq, tr, u_ref, gate_ref, cs_ref, dl_ref, wf_ref, o_ref, ab_ref):
    t = pl.program_id(1)

    @pl.when(t == 0)
    def _():
        for c in range(seq // tr):
            rows = pl.ds(c * tr, tr)
            uc = _dot(u_ref[0, rows, :], cs_ref[...])
            ab_ref[pl.ds(c * tr, tr), :] = uc[:, :F_WIDTH].astype(jnp.bfloat16)
            ab_ref[pl.ds(seq + c * tr, tr), :] = uc[:, F_WIDTH:].astype(jnp.bfloat16)

    y = _dot(dl_ref[...], ab_ref[...]).astype(jnp.bfloat16)
    o_ref[0] = (_dot(y, wf_ref[...]) * gate_ref[0].astype(jnp.float32)).astype(jnp.bfloat16)


def _fourier(p, chan_tab, seq_tab, w_four):
    b, seq, _ = p.shape
    tr = min(512, seq)
    return pl.pallas_call(
        functools.partial(_fourier_kernel, seq, tr),
        grid=(b, seq // tr),
        in_specs=[
            pl.BlockSpec((1, seq, GROUP_W), lambda i, t: (i, 0, 0)),
            pl.BlockSpec((1, tr, GROUP_W), lambda i, t: (i, t, 1)),
            pl.BlockSpec((F_WIDTH, 2 * F_WIDTH), lambda i, t: (0, 0)),
            pl.BlockSpec((tr, 2 * seq), lambda i, t: (t, 0)),
            pl.BlockSpec((F_WIDTH, F_WIDTH), lambda i, t: (0, 0)),
        ],
        out_specs=pl.BlockSpec((1, tr, F_WIDTH), lambda i, t: (i, t, 0)),
        out_shape=jax.ShapeDtypeStruct((b, seq, F_WIDTH), jnp.bfloat16),
        scratch_shapes=[pltpu.VMEM((2 * seq, F_WIDTH), jnp.bfloat16)],
        compiler_params=_cparams("parallel", "arbitrary"),
        name="fourier",
    )(p, p, chan_tab, seq_tab, w_four)


def _dft_tables(seq):
    n = np.arange(seq, dtype=np.int64)
    ang = 2.0 * np.pi * ((n[:, None] * n[None, :]) % seq) / seq
    seq_tab = np.concatenate([np.cos(ang), -np.sin(ang)], axis=1)
    m = np.arange(F_GROUP_DIM, dtype=np.int64)
    angc = 2.0 * np.pi * ((m[:, None] * m[None, :]) % F_GROUP_DIM) / F_GROUP_DIM
    eye = np.eye(F_GROUPS)
    scale = (seq * F_GROUP_DIM) ** -0.5
    chan_tab = np.concatenate([np.kron(eye, np.cos(angc)), np.kron(eye, np.sin(angc))], axis=1) * scale
    return (jnp.asarray(chan_tab, jnp.float32).astype(jnp.bfloat16),
            jnp.asarray(seq_tab, jnp.float32).astype(jnp.bfloat16))


def _quad_attention(qq, k_loc, v_loc, k_ctx, v_ctx, bias):
    n = qq.shape[0]
    head_of_lane = jax.lax.broadcasted_iota(jnp.int32, (n, QUAD_W), 1) // HEAD_DIM
    zero = jnp.zeros_like(qq)
    lhs = jnp.concatenate([jnp.where(head_of_lane == h, qq, zero) for h in range(HEADS_PER_QUAD)], axis=0)
    s_ctx = _dot_nt(lhs, k_ctx)
    m = jnp.max(s_ctx, axis=-1, keepdims=True)
    if k_loc is not None:
        s_loc = _dot_nt(lhs, k_loc) + bias
        m = jnp.maximum(m, jnp.max(s_loc, axis=-1, keepdims=True))
        e_loc = jnp.exp(s_loc - m)
    e_ctx = jnp.exp(s_ctx - m)
    denom = jnp.sum(e_ctx, axis=-1, keepdims=True)
    o = _dot(e_ctx.astype(jnp.bfloat16), v_ctx)
    if k_loc is not None:
        denom = denom + jnp.sum(e_loc, axis=-1, keepdims=True)
        o = o + _dot(e_loc.astype(jnp.bfloat16), v_loc)
    o = o * (1.0 / denom)
    out = o[:n]
    for h in range(1, HEADS_PER_QUAD):
        out = jnp.where(head_of_lane == h, o[h * n:(h + 1) * n], out)
    return out


def _nattn_kernel(rows_per_step, q_ref, g_ref, k_ref, v_ref, kc_ref, vc_ref, bias_ref, o_ref):
    t = pl.program_id(1)
    n_rows = k_ref.shape[1] // GRID_W

    def row_body(i, carry):
        r = t * rows_per_step + i
        rs = jnp.clip(r - WIN_ROWS // 2, 0, n_rows - WIN_ROWS)
        pat = jnp.minimum(r, WIN_ROWS // 2) + jnp.maximum(r - (n_rows - WIN_ROWS // 2), 0)
        qrows = pl.ds(pl.multiple_of(i * GRID_W, GRID_W), GRID_W)
        band = pl.ds(pl.multiple_of(rs * GRID_W, GRID_W), WIN_ROWS * GRID_W)
        for quad in range(N_QUADS):
            lanes = slice(quad * QUAD_W, (quad + 1) * QUAD_W)
            out = _quad_attention(
                q_ref[0, qrows, lanes],
                k_ref[0, band, lanes], v_ref[0, band, lanes],
                kc_ref[0, :, lanes], vc_ref[0, :, lanes],
                bias_ref[pat, quad].astype(jnp.float32))
            o_ref[0, qrows, lanes] = (out * g_ref[0, qrows, lanes].astype(jnp.float32)).astype(jnp.bfloat16)
        return carry

    jax.lax.fori_loop(0, rows_per_step, row_body, 0)


def _nattn(px, pc, bias):
    b, seq, _ = px.shape
    lc = pc.shape[1]
    rows_per_step = 8
    tq = rows_per_step * GRID_W
    return pl.pallas_call(
        functools.partial(_nattn_kernel, rows_per_step),
        grid=(b, seq // tq),
        in_specs=[
            pl.BlockSpec((1, tq, GROUP_W), lambda i, t: (i, t, 2)),
            pl.BlockSpec((1, tq, GROUP_W), lambda i, t: (i, t, 5)),
            pl.BlockSpec((1, seq, GROUP_W), lambda i, t: (i, 0, 3)),
            pl.BlockSpec((1, seq, GROUP_W), lambda i, t: (i, 0, 4)),
            pl.BlockSpec((1, lc, GROUP_W), lambda i, t: (i, 0, 3)),
            pl.BlockSpec((1, lc, GROUP_W), lambda i, t: (i, 0, 4)),
            pl.BlockSpec((N_BIAS_PATTERNS, N_QUADS, HEADS_PER_QUAD * GRID_W, WIN_ROWS * GRID_W),
                         lambda i, t: (0, 0, 0, 0)),
        ],
        out_specs=pl.BlockSpec((1, tq, NA_WIDTH), lambda i, t: (i, t, 0)),
        out_shape=jax.ShapeDtypeStruct((b, seq, NA_WIDTH), jnp.bfloat16),
        compiler_params=_cparams("parallel", "parallel"),
        name="nattn",
    )(px, px, px, px, pc, pc, bias)


def _cattn_kernel(q_ref, g_ref, k_ref, v_ref, o_ref):
    for quad in range(N_QUADS):
        lanes = slice(quad * QUAD_W, (quad + 1) * QUAD_W)
        out = _quad_attention(q_ref[0, :, lanes], None, None, k_ref[0, :, lanes], v_ref[0, :, lanes], None)
        o_ref[0, :, lanes] = (out * g_ref[0, :, lanes].astype(jnp.float32)).astype(jnp.bfloat16)


def _cattn(pc):
    b, lc, _ = pc.shape
    spec = lambda j: pl.BlockSpec((1, lc, GROUP_W), lambda i: (i, 0, j))
    return pl.pallas_call(
        _cattn_kernel,
        grid=(b,),
        in_specs=[spec(2), spec(5), spec(3), spec(4)],
        out_specs=pl.BlockSpec((1, lc, NA_WIDTH), lambda i: (i, 0, 0)),
        out_shape=jax.ShapeDtypeStruct((b, lc, NA_WIDTH), jnp.bfloat16),
        compiler_params=_cparams("parallel"),
        name="cattn",
    )(pc, pc, pc, pc)


def _bias_tables(rel_bias, n_rows):
    half = WIN_ROWS // 2
    rep_rows = np.array(list(range(half)) + [half] + list(range(n_rows - half + 1, n_rows)))
    rs = np.clip(rep_rows - half, 0, n_rows - WIN_ROWS)
    dr_idx = rs[:, None] + np.arange(WIN_ROWS)[None, :] - rep_rows[:, None] + (WIN_ROWS - 1)
    col = np.arange(GRID_W)
    cs = np.clip(col - WIN_COLS // 2, 0, GRID_W - WIN_COLS)
    in_win = (col[None, :] >= cs[:, None]) & (col[None, :] < cs[:, None] + WIN_COLS)
    dc_idx = np.clip(col[None, :] - col[:, None], -(WIN_COLS - 1), WIN_COLS - 1) + (WIN_COLS - 1)
    onehot = (np.arange(2 * WIN_COLS - 1)[:, None, None] == dc_idx[None]).astype(np.float32)
    by_dr = jnp.einsum("lhrc,cqk->lhrqk", rel_bias, jnp.asarray(onehot), precision=jax.lax.Precision.HIGHEST)
    by_dr = jnp.where(in_win[None, None, None], by_dr, NEG_INF)
    tab = jnp.stack([by_dr[:, :, s:s + WIN_ROWS] for s in dr_idx[:, 0]], axis=1)
    tab = jnp.transpose(tab, (0, 1, 2, 4, 3, 5))
    tab = tab.reshape(DEPTH, N_BIAS_PATTERNS, N_QUADS, HEADS_PER_QUAD * GRID_W, WIN_ROWS * GRID_W)
    return tab.astype(jnp.bfloat16)


def kernel(x, c, ctx, c_ctx, norm_g, w_ada, b_ada, w_in, w_four, q_norm_g, k_norm_g, rel_bias, w_out):
    batch, seq, d = x.shape
    bf16 = jnp.bfloat16

    cond = jnp.zeros((MOD_ROWS, d), jnp.float32).at[:batch].set(c).at[batch].set(c_ctx)
    mods = _modulation(cond, w_ada, b_ada)
    mods = mods.reshape(DEPTH, MOD_ROWS, 1, 3 * d)

    w_in_b, w_four_b, w_out_b = w_in.astype(bf16), w_four.astype(bf16), w_out.astype(bf16)
    qg = jnp.tile(q_norm_g, (1, NA_HEADS)) * (HEAD_DIM ** -0.5)
    kg = jnp.tile(k_norm_g, (1, NA_HEADS))
    head_mean = jnp.asarray(np.kron(np.eye(NA_HEADS), np.full((HEAD_DIM, HEAD_DIM), 1.0 / HEAD_DIM)), bf16)
    bias = _bias_tables(rel_bias, seq // GRID_W)
    chan_x, seq_x = _dft_tables(seq)
    chan_c, seq_c = _dft_tables(ctx.shape[1])

    x_mod = lambda i: i
    c_mod = lambda i: batch

    def in_proj_args(l):
        return (mods[l], norm_g[l].reshape(1, d), w_in_b[l], head_mean,
                qg[l].reshape(1, NA_WIDTH), kg[l].reshape(1, NA_WIDTH))

    (px,) = _stream(x, x_mod, nxt=in_proj_args(0))
    (pc,) = _stream(ctx, c_mod, nxt=in_proj_args(0))
    for l in range(DEPTH):
        fo_x = _fourier(px, chan_x, seq_x, w_four_b[l])
        na_x = _nattn(px, pc, bias[l])
        if l + 1 < DEPTH:
            fo_c = _fourier(pc, chan_c, seq_c, w_four_b[l])
            na_c = _cattn(pc)
            x, px = _stream(x, x_mod, prev=(fo_x, na_x, mods[l], w_out_b[l]), nxt=in_proj_args(l + 1))
            ctx, pc = _stream(ctx, c_mod, prev=(fo_c, na_c, mods[l], w_out_b[l]), nxt=in_proj_args(l + 1))
        else:
            (x,) = _stream(x, x_mod, prev=(fo_x, na_x, mods[l], w_out_b[l]))
    return x
```

```python
import functools

import numpy as np
import jax
import jax.numpy as jnp
from jax.experimental import pallas as pl
from jax.experimental.pallas import tpu as pltpu

D_MODEL = 1024
DEPTH = 4
GRID_W = 64
F_WIDTH = 512
F_GROUPS = 4
F_GROUP_DIM = F_WIDTH // F_GROUPS
HEAD_DIM = 64
NA_WIDTH = 512
NA_HEADS = NA_WIDTH // HEAD_DIM
IN_COLS = 2 * F_WIDTH + 4 * NA_WIDTH
WIN_ROWS = 8
WIN_COLS = 16
EPS = 1e-6
NEG_INF = -1e30
LOG2E = 1.4426950408889634

GROUP_W = 512
N_GROUPS = IN_COLS // GROUP_W
QUAD_W = 256
HEADS_PER_QUAD = QUAD_W // HEAD_DIM
N_QUADS = NA_WIDTH // QUAD_W
LANES = 128
N_ROW_OFFSETS = 2 * WIN_ROWS - 1
MOD_ROWS = 16

VMEM_LIMIT = 56 * 1024 * 1024


def _cparams(*sem):
    return pltpu.CompilerParams(dimension_semantics=sem, vmem_limit_bytes=VMEM_LIMIT)


def _dot(a, b):
    return jnp.dot(a, b, preferred_element_type=jnp.float32)


def _dot_nt(a, b):
    return jax.lax.dot_general(a, b, (((1,), (1,)), ((), ())), preferred_element_type=jnp.float32)


def _silu(t):
    return t * (1.0 / (1.0 + jnp.exp(-t)))


def _mod_kernel(cond_ref, w_ref, b_ref, o_ref):
    cond = cond_ref[...]
    o_ref[0] = _dot(_silu(cond), w_ref[0]) + b_ref[0]


def _modulation(cond, w_ada, b_ada):
    tn = 1024
    return pl.pallas_call(
        _mod_kernel,
        grid=(DEPTH, 3 * D_MODEL // tn),
        in_specs=[
            pl.BlockSpec((MOD_ROWS, D_MODEL), lambda l, j: (0, 0)),
            pl.BlockSpec((1, D_MODEL, tn), lambda l, j: (l, 0, j)),
            pl.BlockSpec((1, 1, tn), lambda l, j: (l, 0, j)),
        ],
        out_specs=pl.BlockSpec((1, MOD_ROWS, tn), lambda l, j: (l, 0, j)),
        out_shape=jax.ShapeDtypeStruct((DEPTH, MOD_ROWS, 3 * D_MODEL), jnp.float32),
        compiler_params=_cparams("parallel", "parallel"),
        name="adaln_mod",
    )(cond, w_ada, b_ada.reshape(DEPTH, 1, 3 * D_MODEL))


def _stream_kernel(has_prev, has_next, *refs):
    refs = list(refs)
    x_ref = refs.pop(0)
    if has_prev:
        fo_ref, na_ref, gate_ref, wout_ref = refs[:4]
        refs = refs[4:]
    if has_next:
        mod_ref, ng_ref, win_ref, hm_ref, qg_ref, kg_ref = refs[:6]
        refs = refs[6:]
    if has_prev:
        xo_ref = refs.pop(0)
    if has_next:
        p_ref = refs.pop(0)

    x = x_ref[0]
    if has_prev:
        mix = _dot(fo_ref[0], wout_ref[:F_WIDTH, :]) + _dot(na_ref[0], wout_ref[F_WIDTH:, :])
        x = x + gate_ref[0] * mix
        xo_ref[0] = x
    if has_next:
        ms = jnp.mean(x * x, axis=-1, keepdims=True)
        y = x * jax.lax.rsqrt(ms + EPS) * ng_ref[...]
        shift = mod_ref[0, :, :D_MODEL]
        scale = mod_ref[0, :, D_MODEL:2 * D_MODEL]
        y = (y * (1.0 + scale) + shift).astype(jnp.bfloat16)
        for j in range(N_GROUPS):
            pj = _dot(y, win_ref[:, j * GROUP_W:(j + 1) * GROUP_W])
            if j in (1, 5):
                pj = _silu(pj)
            elif j in (2, 3):
                sq = (pj * pj).astype(jnp.bfloat16)
                ms_h = _dot(sq, hm_ref[...])
                g = qg_ref[...] if j == 2 else kg_ref[...]
                pj = pj * jax.lax.rsqrt(ms_h + EPS) * g
            p_ref[0, :, j * GROUP_W:(j + 1) * GROUP_W] = pj.astype(jnp.bfloat16)


def _stream(x, mod_index, params, prev=None, nxt=None):
    b, l, d = x.shape
    tm = min(512, l)
    in_specs = [pl.BlockSpec((1, tm, d), lambda i, t: (i, t, 0))]
    args = [x]
    out_specs, out_shape = [], []
    if prev is not None:
        lp, f_out, na_out = prev
        in_specs += [
            pl.BlockSpec((1, tm, F_WIDTH), lambda i, t: (i, t, 0)),
            pl.BlockSpec((1, tm, NA_WIDTH), lambda i, t: (i, t, 0)),
            pl.BlockSpec((None, 1, 1, d), lambda i, t: (lp, mod_index(i), 0, 2)),
            pl.BlockSpec((None, d, d), lambda i, t: (lp, 0, 0)),
        ]
        args += [f_out, na_out, params["mods"], params["w_out"]]
        out_specs.append(pl.BlockSpec((1, tm, d), lambda i, t: (i, t, 0)))
        out_shape.append(jax.ShapeDtypeStruct((b, l, d), jnp.float32))
    if nxt is not None:
        ln = nxt
        in_specs += [
            pl.BlockSpec((None, 1, 1, 3 * d), lambda i, t: (ln, mod_index(i), 0, 0)),
            pl.BlockSpec((None, 1, d), lambda i, t: (ln, 0, 0)),
            pl.BlockSpec((None, d, IN_COLS), lambda i, t: (ln, 0, 0)),
            pl.BlockSpec((GROUP_W, GROUP_W), lambda i, t: (0, 0)),
            pl.BlockSpec((None, 1, GROUP_W), lambda i, t: (ln, 0, 0)),
            pl.BlockSpec((None, 1, GROUP_W), lambda i, t: (ln, 0, 0)),
        ]
        args += [params["mods"], params["norm_g"], params["w_in"], params["head_mean"], params["qg"], params["kg"]]
        out_specs.append(pl.BlockSpec((1, tm, IN_COLS), lambda i, t: (i, t, 0)))
        out_shape.append(jax.ShapeDtypeStruct((b, l, IN_COLS), jnp.bfloat16))
    outs = pl.pallas_call(
        functools.partial(_stream_kernel, prev is not None, nxt is not None),
        grid=(b, l // tm),
        in_specs=in_specs,
        out_specs=out_specs,
        out_shape=out_shape,
        compiler_params=_cparams("parallel", "parallel"),
        name="stream_" + ("o" if prev is not None else "") + ("i" if nxt is not None else ""),
    )(*args)
    return outs


def _fourier_kernel(seq, tr, u_ref, gate_ref, cs_ref, dl_ref, wf_ref, o_ref, ab_ref):
    t = pl.program_id(1)

    @pl.when(t == 0)
    def _():
        for c in range(seq // tr):
            rows = pl.ds(c * tr, tr)
            uc = _dot(u_ref[0, rows, :], cs_ref[...])
            ab_ref[pl.ds(c * tr, tr), :] = uc[:, :F_WIDTH].astype(jnp.bfloat16)
            ab_ref[pl.ds(seq + c * tr, tr), :] = uc[:, F_WIDTH:].astype(jnp.bfloat16)

    y = _dot(dl_ref[...], ab_ref[...]).astype(jnp.bfloat16)
    o_ref[0] = (_dot(y, wf_ref[...]) * gate_ref[0].astype(jnp.float32)).astype(jnp.bfloat16)


def _fourier(p, chan_tab, seq_tab, w_four, layer):
    b, seq, _ = p.shape
    tr = min(512, seq)
    return pl.pallas_call(
        functools.partial(_fourier_kernel, seq, tr),
        grid=(b, seq // tr),
        in_specs=[
            pl.BlockSpec((1, seq, GROUP_W), lambda i, t: (i, 0, 0)),
            pl.BlockSpec((1, tr, GROUP_W), lambda i, t: (i, t, 1)),
            pl.BlockSpec((F_WIDTH, 2 * F_WIDTH), lambda i, t: (0, 0)),
            pl.BlockSpec((tr, 2 * seq), lambda i, t: (t, 0)),
            pl.BlockSpec((None, F_WIDTH, F_WIDTH), lambda i, t: (layer, 0, 0)),
        ],
        out_specs=pl.BlockSpec((1, tr, F_WIDTH), lambda i, t: (i, t, 0)),
        out_shape=jax.ShapeDtypeStruct((b, seq, F_WIDTH), jnp.bfloat16),
        scratch_shapes=[pltpu.VMEM((2 * seq, F_WIDTH), jnp.bfloat16)],
        compiler_params=_cparams("parallel", "arbitrary"),
        name="fourier",
    )(p, p, chan_tab, seq_tab, w_four)


def _dft_tables(seq):
    n = np.arange(seq, dtype=np.int64)
    ang = 2.0 * np.pi * ((n[:, None] * n[None, :]) % seq) / seq
    seq_tab = np.concatenate([np.cos(ang), -np.sin(ang)], axis=1)
    m = np.arange(F_GROUP_DIM, dtype=np.int64)
    angc = 2.0 * np.pi * ((m[:, None] * m[None, :]) % F_GROUP_DIM) / F_GROUP_DIM
    eye = np.eye(F_GROUPS)
    scale = (seq * F_GROUP_DIM) ** -0.5
    chan_tab = np.concatenate([np.kron(eye, np.cos(angc)), np.kron(eye, np.sin(angc))], axis=1) * scale
    return (jnp.asarray(chan_tab, jnp.float32).astype(jnp.bfloat16),
            jnp.asarray(seq_tab, jnp.float32).astype(jnp.bfloat16))


def _quad_attention(qq, k_loc, v_loc, k_ctx, v_ctx, bias):
    n = qq.shape[0]
    head_of_lane = jax.lax.broadcasted_iota(jnp.int32, (n, QUAD_W), 1) // HEAD_DIM
    zero = jnp.zeros_like(qq)
    lhs = jnp.concatenate([jnp.where(head_of_lane == h, qq, zero) for h in range(HEADS_PER_QUAD)], axis=0)
    s_ctx = _dot_nt(lhs, k_ctx)
    m = jnp.max(s_ctx, axis=-1, keepdims=True)
    if k_loc is not None:
        s_loc = _dot_nt(lhs, k_loc) + bias
        m = jnp.maximum(m, jnp.max(s_loc, axis=-1, keepdims=True))
        e_loc = jnp.exp2(s_loc - m)
    e_ctx = jnp.exp2(s_ctx - m)
    denom = jnp.sum(e_ctx, axis=-1, keepdims=True)
    o = _dot(e_ctx.astype(jnp.bfloat16), v_ctx)
    if k_loc is not None:
        denom = denom + jnp.sum(e_loc, axis=-1, keepdims=True)
        o = o + _dot(e_loc.astype(jnp.bfloat16), v_loc)
    o = o * (1.0 / denom)
    out = o[:n]
    for h in range(1, HEADS_PER_QUAD):
        out = jnp.where(head_of_lane == h, o[h * n:(h + 1) * n], out)
    return out


def _nattn_kernel(rows_per_step, q_ref, g_ref, k_ref, v_ref, kc_ref, vc_ref, bias_ref, o_ref):
    t = pl.program_id(1)
    n_rows = k_ref.shape[1] // GRID_W
    half = WIN_ROWS // 2

    for i in range(rows_per_step):
        r = t * rows_per_step + i
        rs = jnp.clip(r - half, 0, n_rows - WIN_ROWS)
        first = (WIN_ROWS - 1) - jnp.minimum(r, half) - jnp.maximum(r - (n_rows - half), 0)
        qrows = pl.ds(i * GRID_W, GRID_W)
        band = pl.ds(pl.multiple_of(rs * GRID_W, GRID_W), WIN_ROWS * GRID_W)
        for quad in range(N_QUADS):
            lanes = slice(quad * QUAD_W, (quad + 1) * QUAD_W)
            bias = jnp.concatenate([
                jnp.concatenate([bias_ref[quad * HEADS_PER_QUAD + h, first + 2 * m]
                                 for m in range(WIN_ROWS // 2)], axis=1)
                for h in range(HEADS_PER_QUAD)], axis=0)
            out = _quad_attention(
                q_ref[0, qrows, lanes],
                k_ref[0, band, lanes], v_ref[0, band, lanes],
                kc_ref[0, :, lanes], vc_ref[0, :, lanes], bias)
            o_ref[0, qrows, lanes] = (out * g_ref[0, qrows, lanes].astype(jnp.float32)).astype(jnp.bfloat16)


def _nattn(px, pc, bias, layer):
    b, seq, _ = px.shape
    lc = pc.shape[1]
    rows_per_step = 8
    tq = rows_per_step * GRID_W
    return pl.pallas_call(
        functools.partial(_nattn_kernel, rows_per_step),
        grid=(b, seq // tq),
        in_specs=[
            pl.BlockSpec((1, tq, GROUP_W), lambda i, t: (i, t, 2)),
            pl.BlockSpec((1, tq, GROUP_W), lambda i, t: (i, t, 5)),
            pl.BlockSpec((1, seq, GROUP_W), lambda i, t: (i, 0, 3)),
            pl.BlockSpec((1, seq, GROUP_W), lambda i, t: (i, 0, 4)),
            pl.BlockSpec((1, lc, GROUP_W), lambda i, t: (i, 0, 3)),
            pl.BlockSpec((1, lc, GROUP_W), lambda i, t: (i, 0, 4)),
            pl.BlockSpec((None, NA_HEADS, N_ROW_OFFSETS - 1, GRID_W, LANES), lambda i, t: (layer, 0, 0, 0, 0)),
        ],
        out_specs=pl.BlockSpec((1, tq, NA_WIDTH), lambda i, t: (i, t, 0)),
        out_shape=jax.ShapeDtypeStruct((b, seq, NA_WIDTH), jnp.bfloat16),
        compiler_params=_cparams("parallel", "parallel"),
        name="nattn",
    )(px, px, px, px, pc, pc, bias)


def _cattn_kernel(q_ref, g_ref, k_ref, v_ref, o_ref):
    for quad in range(N_QUADS):
        lanes = slice(quad * QUAD_W, (quad + 1) * QUAD_W)
        out = _quad_attention(q_ref[0, :, lanes], None, None, k_ref[0, :, lanes], v_ref[0, :, lanes], None)
        o_ref[0, :, lanes] = (out * g_ref[0, :, lanes].astype(jnp.float32)).astype(jnp.bfloat16)


def _cattn(pc):
    b, lc, _ = pc.shape
    spec = lambda j: pl.BlockSpec((1, lc, GROUP_W), lambda i: (i, 0, j))
    return pl.pallas_call(
        _cattn_kernel,
        grid=(b,),
        in_specs=[spec(2), spec(5), spec(3), spec(4)],
        out_specs=pl.BlockSpec((1, lc, NA_WIDTH), lambda i: (i, 0, 0)),
        out_shape=jax.ShapeDtypeStruct((b, lc, NA_WIDTH), jnp.bfloat16),
        compiler_params=_cparams("parallel"),
        name="cattn",
    )(pc, pc, pc, pc)


def _bias_tables(rel_bias):
    col = np.arange(GRID_W)
    cs = np.clip(col - WIN_COLS // 2, 0, GRID_W - WIN_COLS)
    in_win = (col[None, :] >= cs[:, None]) & (col[None, :] < cs[:, None] + WIN_COLS)
    dc_idx = np.clip(col[None, :] - col[:, None], -(WIN_COLS - 1), WIN_COLS - 1) + (WIN_COLS - 1)
    onehot = (np.arange(2 * WIN_COLS - 1)[:, None, None] == dc_idx[None]).astype(np.float32)
    by_dr = jnp.einsum("lhrc,cqk->lhrqk", rel_bias, jnp.asarray(onehot), precision=jax.lax.Precision.HIGHEST)
    by_dr = jnp.where(in_win[None, None, None], by_dr * LOG2E, NEG_INF)
    return jnp.concatenate([by_dr[:, :, :-1], by_dr[:, :, 1:]], axis=-1)


def kernel(x, c, ctx, c_ctx, norm_g, w_ada, b_ada, w_in, w_four, q_norm_g, k_norm_g, rel_bias, w_out):
    batch, seq, d = x.shape
    bf16 = jnp.bfloat16

    cond = jnp.zeros((MOD_ROWS, d), jnp.float32).at[:batch].set(c).at[batch].set(c_ctx)
    mods = _modulation(cond, w_ada, b_ada)

    params = {
        "mods": mods.reshape(DEPTH, MOD_ROWS, 1, 3 * d),
        "norm_g": norm_g.reshape(DEPTH, 1, d),
        "w_in": w_in.astype(bf16),
        "w_out": w_out.astype(bf16),
        "qg": (jnp.tile(q_norm_g, (1, NA_HEADS)) * (HEAD_DIM ** -0.5 * LOG2E)).reshape(DEPTH, 1, NA_WIDTH),
        "kg": jnp.tile(k_norm_g, (1, NA_HEADS)).reshape(DEPTH, 1, NA_WIDTH),
        "head_mean": jnp.asarray(
            np.kron(np.eye(NA_HEADS), np.full((HEAD_DIM, HEAD_DIM), 1.0 / HEAD_DIM)), jnp.float32).astype(bf16),
    }
    w_four_b = w_four.astype(bf16)
    bias = _bias_tables(rel_bias)
    chan_x, seq_x = _dft_tables(seq)
    chan_c, seq_c = _dft_tables(ctx.shape[1])

    x_mod = lambda i: i
    c_mod = lambda i: batch

    (px,) = _stream(x, x_mod, params, nxt=0)
    (pc,) = _stream(ctx, c_mod, params, nxt=0)
    for l in range(DEPTH):
        fo_x = _fourier(px, chan_x, seq_x, w_four_b, l)
        na_x = _nattn(px, pc, bias, l)
        if l + 1 < DEPTH:
            fo_c = _fourier(pc, chan_c, seq_c, w_four_b, l)
            na_c = _cattn(pc)
            x, px = _stream(x, x_mod, params, prev=(l, fo_x, na_x), nxt=l + 1)
            ctx, pc = _stream(ctx, c_mod, params, prev=(l, fo_c, na_c), nxt=l + 1)
        else:
            (x,) = _stream(x, x_mod, params, prev=(l, fo_x, na_x))
    return x
```

```python
import functools

import numpy as np
import jax
import jax.numpy as jnp
from jax.experimental import pallas as pl
from jax.experimental.pallas import tpu as pltpu

D_MODEL = 1024
DEPTH = 4
GRID_W = 64
F_WIDTH = 512
F_GROUPS = 4
F_GROUP_DIM = F_WIDTH // F_GROUPS
HEAD_DIM = 64
NA_WIDTH = 512
NA_HEADS = NA_WIDTH // HEAD_DIM
IN_COLS = 2 * F_WIDTH + 4 * NA_WIDTH
WIN_ROWS = 8
WIN_COLS = 16
EPS = 1e-6
NEG_INF = -1e30
LOG2E = 1.4426950408889634

GROUP_W = 512
N_GROUPS = IN_COLS // GROUP_W
QUAD_W = 256
HEADS_PER_QUAD = QUAD_W // HEAD_DIM
N_QUADS = NA_WIDTH // QUAD_W
LANES = 128
N_ROW_OFFSETS = 2 * WIN_ROWS - 1
MOD_ROWS = 16

VMEM_LIMIT = 56 * 1024 * 1024


def _cparams(*sem):
    return pltpu.CompilerParams(dimension_semantics=sem, vmem_limit_bytes=VMEM_LIMIT)


def _dot(a, b):
    return jnp.dot(a, b, preferred_element_type=jnp.float32)


def _dot_nt(a, b):
    return jax.lax.dot_general(a, b, (((1,), (1,)), ((), ())), preferred_element_type=jnp.float32)


def _silu(t):
    return t * (1.0 / (1.0 + jnp.exp(-t)))


def _mod_kernel(cond_ref, w_ref, b_ref, o_ref):
    cond = cond_ref[...]
    o_ref[0] = _dot(_silu(cond), w_ref[0]) + b_ref[0]


def _modulation(cond, w_ada, b_ada):
    tn = 1024
    return pl.pallas_call(
        _mod_kernel,
        grid=(DEPTH, 3 * D_MODEL // tn),
        in_specs=[
            pl.BlockSpec((MOD_ROWS, D_MODEL), lambda l, j: (0, 0)),
            pl.BlockSpec((1, D_MODEL, tn), lambda l, j: (l, 0, j)),
            pl.BlockSpec((1, 1, tn), lambda l, j: (l, 0, j)),
        ],
        out_specs=pl.BlockSpec((1, MOD_ROWS, tn), lambda l, j: (l, 0, j)),
        out_shape=jax.ShapeDtypeStruct((DEPTH, MOD_ROWS, 3 * D_MODEL), jnp.float32),
        compiler_params=_cparams("parallel", "parallel"),
        name="adaln_mod",
    )(cond, w_ada, b_ada.reshape(DEPTH, 1, 3 * D_MODEL))


def _stream_kernel(has_prev, has_next, *refs):
    refs = list(refs)
    x_ref = refs.pop(0)
    if has_prev:
        fo_ref, na_ref, gate_ref, wout_ref = refs[:4]
        refs = refs[4:]
    if has_next:
        mod_ref, ng_ref, win_ref, hm_ref, qg_ref, kg_ref = refs[:6]
        refs = refs[6:]
    if has_prev:
        xo_ref = refs.pop(0)
    if has_next:
        p_ref = refs.pop(0)

    x = x_ref[0]
    if has_prev:
        mix = _dot(fo_ref[0], wout_ref[:F_WIDTH, :]) + _dot(na_ref[0], wout_ref[F_WIDTH:, :])
        x = x + gate_ref[0] * mix
        xo_ref[0] = x
    if has_next:
        ms = jnp.mean(x * x, axis=-1, keepdims=True)
        y = x * jax.lax.rsqrt(ms + EPS) * ng_ref[...]
        shift = mod_ref[0, :, :D_MODEL]
        scale = mod_ref[0, :, D_MODEL:2 * D_MODEL]
        y = (y * (1.0 + scale) + shift).astype(jnp.bfloat16)
        for j in range(N_GROUPS):
            pj = _dot(y, win_ref[:, j * GROUP_W:(j + 1) * GROUP_W])
            if j in (1, 5):
                pj = _silu(pj)
            elif j in (2, 3):
                sq = (pj * pj).astype(jnp.bfloat16)
                ms_h = jnp.concatenate([_dot(sq[:, q * QUAD_W:(q + 1) * QUAD_W], hm_ref[...])
                                        for q in range(N_QUADS)], axis=1)
                g = qg_ref[...] if j == 2 else kg_ref[...]
                pj = pj * jax.lax.rsqrt(ms_h + EPS) * g
            p_ref[0, :, j * GROUP_W:(j + 1) * GROUP_W] = pj.astype(jnp.bfloat16)


def _stream(x, mod_index, params, prev=None, nxt=None):
    b, l, d = x.shape
    tm = min(512, l)
    in_specs = [pl.BlockSpec((1, tm, d), lambda i, t: (i, t, 0))]
    args = [x]
    out_specs, out_shape = [], []
    if prev is not None:
        lp, f_out, na_out = prev
        in_specs += [
            pl.BlockSpec((1, tm, F_WIDTH), lambda i, t: (i, t, 0)),
            pl.BlockSpec((1, tm, NA_WIDTH), lambda i, t: (i, t, 0)),
            pl.BlockSpec((None, 1, 1, d), lambda i, t: (lp, mod_index(i), 0, 2)),
            pl.BlockSpec((None, d, d), lambda i, t: (lp, 0, 0)),
        ]
        args += [f_out, na_out, params["mods"], params["w_out"]]
        out_specs.append(pl.BlockSpec((1, tm, d), lambda i, t: (i, t, 0)))
        out_shape.append(jax.ShapeDtypeStruct((b, l, d), jnp.float32))
    if nxt is not None:
        ln = nxt
        in_specs += [
            pl.BlockSpec((None, 1, 1, 3 * d), lambda i, t: (ln, mod_index(i), 0, 0)),
            pl.BlockSpec((None, 1, d), lambda i, t: (ln, 0, 0)),
            pl.BlockSpec((None, d, IN_COLS), lambda i, t: (ln, 0, 0)),
            pl.BlockSpec((QUAD_W, QUAD_W), lambda i, t: (0, 0)),
            pl.BlockSpec((None, 1, GROUP_W), lambda i, t: (ln, 0, 0)),
            pl.BlockSpec((None, 1, GROUP_W), lambda i, t: (ln, 0, 0)),
        ]
        args += [params["mods"], params["norm_g"], params["w_in"], params["head_mean"], params["qg"], params["kg"]]
        out_specs.append(pl.BlockSpec((1, tm, IN_COLS), lambda i, t: (i, t, 0)))
        out_shape.append(jax.ShapeDtypeStruct((b, l, IN_COLS), jnp.bfloat16))
    outs = pl.pallas_call(
        functools.partial(_stream_kernel, prev is not None, nxt is not None),
        grid=(b, l // tm),
        in_specs=in_specs,
        out_specs=out_specs,
        out_shape=out_shape,
        compiler_params=_cparams("parallel", "parallel"),
        name="stream_" + ("o" if prev is not None else "") + ("i" if nxt is not None else ""),
    )(*args)
    return outs


def _fourier_kernel(seq, tr, u_ref, gate_ref, cs_ref, dl_ref, wf_ref, o_ref, ab_ref):
    t = pl.program_id(1)

    @pl.when(t == 0)
    def _():
        for c in range(seq // tr):
            rows = pl.ds(c * tr, tr)
            uc = _dot(u_ref[0, rows, :], cs_ref[...])
            ab_ref[pl.ds(c * tr, tr), :] = uc[:, :F_WIDTH].astype(jnp.bfloat16)
            ab_ref[pl.ds(seq + c * tr, tr), :] = uc[:, F_WIDTH:].astype(jnp.bfloat16)

    y = _dot(dl_ref[...], ab_ref[...]).astype(jnp.bfloat16)
    o_ref[0] = (_dot(y, wf_ref[...]) * gate_ref[0].astype(jnp.float32)).astype(jnp.bfloat16)


def _fourier(p, chan_tab, seq_tab, w_four, layer):
    b, seq, _ = p.shape
    tr = min(512, seq)
    return pl.pallas_call(
        functools.partial(_fourier_kernel, seq, tr),
        grid=(b, seq // tr),
        in_specs=[
            pl.BlockSpec((1, seq, GROUP_W), lambda i, t: (i, 0, 0)),
            pl.BlockSpec((1, tr, GROUP_W), lambda i, t: (i, t, 1)),
            pl.BlockSpec((F_WIDTH, 2 * F_WIDTH), lambda i, t: (0, 0)),
            pl.BlockSpec((tr, 2 * seq), lambda i, t: (t, 0)),
            pl.BlockSpec((None, F_WIDTH, F_WIDTH), lambda i, t: (layer, 0, 0)),
        ],
        out_specs=pl.BlockSpec((1, tr, F_WIDTH), lambda i, t: (i, t, 0)),
        out_shape=jax.ShapeDtypeStruct((b, seq, F_WIDTH), jnp.bfloat16),
        scratch_shapes=[pltpu.VMEM((2 * seq, F_WIDTH), jnp.bfloat16)],
        compiler_params=_cparams("parallel", "arbitrary"),
        name="fourier",
    )(p, p, chan_tab, seq_tab, w_four)


def _dft_tables(seq):
    n = np.arange(seq, dtype=np.int64)
    ang = 2.0 * np.pi * ((n[:, None] * n[None, :]) % seq) / seq
    seq_tab = np.concatenate([np.cos(ang), -np.sin(ang)], axis=1)
    m = np.arange(F_GROUP_DIM, dtype=np.int64)
    angc = 2.0 * np.pi * ((m[:, None] * m[None, :]) % F_GROUP_DIM) / F_GROUP_DIM
    eye = np.eye(F_GROUPS)
    scale = (seq * F_GROUP_DIM) ** -0.5
    chan_tab = np.concatenate([np.kron(eye, np.cos(angc)), np.kron(eye, np.sin(angc))], axis=1) * scale
    return (jnp.asarray(chan_tab, jnp.float32).astype(jnp.bfloat16),
            jnp.asarray(seq_tab, jnp.float32).astype(jnp.bfloat16))


def _complex_dft8(xr, xi):
    c = 0.5 ** 0.5
    sr = [xr[n] + xr[n + 4] for n in range(4)]
    si = [xi[n] + xi[n + 4] for n in range(4)]
    dr = [xr[n] - xr[n + 4] for n in range(4)]
    di = [xi[n] - xi[n + 4] for n in range(4)]
    t0r, t0i, t1r, t1i = sr[0] + sr[2], si[0] + si[2], sr[0] - sr[2], si[0] - si[2]
    t2r, t2i, t3r, t3i = sr[1] + sr[3], si[1] + si[3], sr[1] - sr[3], si[1] - si[3]
    e1r, e1i = c * (dr[1] + di[1]), c * (di[1] - dr[1])
    e2r, e2i = di[2], -dr[2]
    e3r, e3i = c * (di[3] - dr[3]), -c * (dr[3] + di[3])
    u0r, u0i, u1r, u1i = dr[0] + e2r, di[0] + e2i, dr[0] - e2r, di[0] - e2i
    u2r, u2i, u3r, u3i = e1r + e3r, e1i + e3i, e1r - e3r, e1i - e3i
    vr = [t0r + t2r, u0r + u2r, t1r + t3i, u1r + u3i, t0r - t2r, u0r - u2r, t1r - t3i, u1r - u3i]
    vi = [t0i + t2i, u0i + u2i, t1i - t3r, u1i - u3r, t0i - t2i, u0i - u2i, t1i + t3r, u1i + u3r]
    return vr, vi


FFT_N1 = 8
FFT_N2 = 256
FFT_DECIM = FFT_N2 // FFT_N1
FFT_BFLY_ROWS = 32


def _fourier_fft_kernel(u_ref, gate_ref, perm_ref, dft_ref, twc_ref, tws_ref, cs_ref, wf_ref, o_ref,
                        up_ref, br_ref, bi_ref, v_ref):
    bf16 = jnp.bfloat16
    n_nat = u_ref.shape[1] // FFT_N2
    for j in range(n_nat):
        t = _dot(perm_ref[...], u_ref[0, j * FFT_N2:(j + 1) * FFT_N2, :]).astype(bf16)
        for n1 in range(FFT_N1):
            up_ref[n1, j * FFT_DECIM:(j + 1) * FFT_DECIM, :] = t[n1 * FFT_DECIM:(n1 + 1) * FFT_DECIM]
    for n1 in range(FFT_N1):
        b = _dot(dft_ref[...], up_ref[n1])
        br, bi = b[:FFT_N2], b[FFT_N2:]
        if n1 > 0:
            tc = jnp.concatenate([twc_ref[n1]] * (F_WIDTH // LANES), axis=1)
            ts = jnp.concatenate([tws_ref[n1]] * (F_WIDTH // LANES), axis=1)
            br, bi = br * tc + bi * ts, bi * tc - br * ts
        br_ref[n1] = br
        bi_ref[n1] = bi
    for ch in range(FFT_N2 // FFT_BFLY_ROWS):
        rows = slice(ch * FFT_BFLY_ROWS, (ch + 1) * FFT_BFLY_ROWS)
        vr, vi = _complex_dft8([br_ref[n, rows, :] for n in range(FFT_N1)],
                               [bi_ref[n, rows, :] for n in range(FFT_N1)])
        for k1 in range(FFT_N1):
            out_rows = slice(k1 * FFT_N2 + ch * FFT_BFLY_ROWS, k1 * FFT_N2 + (ch + 1) * FFT_BFLY_ROWS)
            v_ref[out_rows, :F_WIDTH] = vr[k1].astype(bf16)
            v_ref[out_rows, F_WIDTH:] = vi[k1].astype(bf16)
    tr = 512
    for t in range(u_ref.shape[1] // tr):
        rows = slice(t * tr, (t + 1) * tr)
        y = _dot(v_ref[rows, :], cs_ref[...]).astype(bf16)
        o_ref[0, rows, :] = (_dot(y, wf_ref[...]) * gate_ref[0, rows, :].astype(jnp.float32)).astype(bf16)


def _fourier_fft(p, tabs, w_four, layer):
    b, seq, _ = p.shape
    assert seq == FFT_N1 * FFT_N2
    const = lambda *shape: pl.BlockSpec(shape, lambda i: (0,) * len(shape))
    return pl.pallas_call(
        _fourier_fft_kernel,
        grid=(b,),
        in_specs=[
            pl.BlockSpec((1, seq, GROUP_W), lambda i: (i, 0, 0)),
            pl.BlockSpec((1, seq, GROUP_W), lambda i: (i, 0, 1)),
            const(FFT_N2, FFT_N2),
            const(2 * FFT_N2, FFT_N2),
            const(FFT_N1, FFT_N2, LANES),
            const(FFT_N1, FFT_N2, LANES),
            const(2 * F_WIDTH, F_WIDTH),
            pl.BlockSpec((None, F_WIDTH, F_WIDTH), lambda i: (layer, 0, 0)),
        ],
        out_specs=pl.BlockSpec((1, seq, F_WIDTH), lambda i: (i, 0, 0)),
        out_shape=jax.ShapeDtypeStruct((b, seq, F_WIDTH), jnp.bfloat16),
        scratch_shapes=[
            pltpu.VMEM((FFT_N1, FFT_N2, F_WIDTH), jnp.bfloat16),
            pltpu.VMEM((FFT_N1, FFT_N2, F_WIDTH), jnp.float32),
            pltpu.VMEM((FFT_N1, FFT_N2, F_WIDTH), jnp.float32),
            pltpu.VMEM((seq, 2 * F_WIDTH), jnp.bfloat16),
        ],
        compiler_params=_cparams("parallel"),
        name="fourier_fft",
    )(p, p, *tabs, w_four)


def _fft_tables(seq):
    assert seq == FFT_N1 * FFT_N2
    rows = np.arange(FFT_N2)
    perm = np.zeros((FFT_N2, FFT_N2))
    for n1 in range(FFT_N1):
        for m in range(FFT_DECIM):
            perm[n1 * FFT_DECIM + m, FFT_N1 * m + n1] = 1.0
    ang = 2.0 * np.pi * ((rows[:, None] * rows[None, :]) % FFT_N2) / FFT_N2
    dft = np.concatenate([np.cos(ang), -np.sin(ang)], axis=0)
    angt = 2.0 * np.pi * (np.arange(FFT_N1)[:, None] * rows[None, :]) / seq
    twc = np.broadcast_to(np.cos(angt)[:, :, None], (FFT_N1, FFT_N2, LANES))
    tws = np.broadcast_to(np.sin(angt)[:, :, None], (FFT_N1, FFT_N2, LANES))
    m = np.arange(F_GROUP_DIM, dtype=np.int64)
    angc = 2.0 * np.pi * ((m[:, None] * m[None, :]) % F_GROUP_DIM) / F_GROUP_DIM
    eye = np.eye(F_GROUPS)
    scale = (seq * F_GROUP_DIM) ** -0.5
    chan = np.concatenate([np.kron(eye, np.cos(angc)), np.kron(eye, np.sin(angc))], axis=0) * scale
    f32 = lambda a: jnp.asarray(np.ascontiguousarray(a), jnp.float32)
    return (f32(perm).astype(jnp.bfloat16), f32(dft).astype(jnp.bfloat16), f32(twc), f32(tws),
            f32(chan).astype(jnp.bfloat16))


def _quad_attention(qq, k_loc, v_loc, k_ctx, v_ctx, bias):
    n = qq.shape[0]
    head_of_lane = jax.lax.broadcasted_iota(jnp.int32, (n, QUAD_W), 1) // HEAD_DIM
    zero = jnp.zeros_like(qq)
    lhs = jnp.concatenate([jnp.where(head_of_lane == h, qq, zero) for h in range(HEADS_PER_QUAD)], axis=0)
    s_ctx = _dot_nt(lhs, k_ctx)
    m = jnp.max(s_ctx, axis=-1, keepdims=True)
    if k_loc is not None:
        s_loc = _dot_nt(lhs, k_loc) + bias
        m = jnp.maximum(m, jnp.max(s_loc, axis=-1, keepdims=True))
        e_loc = jnp.exp2(s_loc - m)
    e_ctx = jnp.exp2(s_ctx - m)
    denom = jnp.sum(e_ctx, axis=-1, keepdims=True)
    o = _dot(e_ctx.astype(jnp.bfloat16), v_ctx)
    if k_loc is not None:
        denom = denom + jnp.sum(e_loc, axis=-1, keepdims=True)
        o = o + _dot(e_loc.astype(jnp.bfloat16), v_loc)
    o = o * (1.0 / denom)
    out = o[:n]
    for h in range(1, HEADS_PER_QUAD):
        out = jnp.where(head_of_lane == h, o[h * n:(h + 1) * n], out)
    return out


def _nattn_kernel(rows_per_step, q_ref, g_ref, k_ref, v_ref, kc_ref, vc_ref, bias_ref, o_ref):
    t = pl.program_id(1)
    n_rows = k_ref.shape[1] // GRID_W
    half = WIN_ROWS // 2

    for i in range(rows_per_step):
        r = t * rows_per_step + i
        rs = jnp.clip(r - half, 0, n_rows - WIN_ROWS)
        first = (WIN_ROWS - 1) - jnp.minimum(r, half) - jnp.maximum(r - (n_rows - half), 0)
        qrows = pl.ds(i * GRID_W, GRID_W)
        band = pl.ds(pl.multiple_of(rs * GRID_W, GRID_W), WIN_ROWS * GRID_W)
        for quad in range(N_QUADS):
            lanes = slice(quad * QUAD_W, (quad + 1) * QUAD_W)
            bias = jnp.concatenate([
                jnp.concatenate([bias_ref[quad * HEADS_PER_QUAD + h, first + 2 * m]
                                 for m in range(WIN_ROWS // 2)], axis=1)
                for h in range(HEADS_PER_QUAD)], axis=0)
            out = _quad_attention(
                q_ref[0, qrows, lanes],
                k_ref[0, band, lanes], v_ref[0, band, lanes],
                kc_ref[0, :, lanes], vc_ref[0, :, lanes], bias)
            o_ref[0, qrows, lanes] = (out * g_ref[0, qrows, lanes].astype(jnp.float32)).astype(jnp.bfloat16)


def _nattn(px, pc, bias, layer):
    b, seq, _ = px.shape
    lc = pc.shape[1]
    rows_per_step = 8
    tq = rows_per_step * GRID_W
    return pl.pallas_call(
        functools.partial(_nattn_kernel, rows_per_step),
        grid=(b, seq // tq),
        in_specs=[
            pl.BlockSpec((1, tq, GROUP_W), lambda i, t: (i, t, 2)),
            pl.BlockSpec((1, tq, GROUP_W), lambda i, t: (i, t, 5)),
            pl.BlockSpec((1, seq, GROUP_W), lambda i, t: (i, 0, 3)),
            pl.BlockSpec((1, seq, GROUP_W), lambda i, t: (i, 0, 4)),
            pl.BlockSpec((1, lc, GROUP_W), lambda i, t: (i, 0, 3)),
            pl.BlockSpec((1, lc, GROUP_W), lambda i, t: (i, 0, 4)),
            pl.BlockSpec((None, NA_HEADS, N_ROW_OFFSETS - 1, GRID_W, LANES), lambda i, t: (layer, 0, 0, 0, 0)),
        ],
        out_specs=pl.BlockSpec((1, tq, NA_WIDTH), lambda i, t: (i, t, 0)),
        out_shape=jax.ShapeDtypeStruct((b, seq, NA_WIDTH), jnp.bfloat16),
        compiler_params=_cparams("parallel", "parallel"),
        name="nattn",
    )(px, px, px, px, pc, pc, bias)


def _cattn_kernel(q_ref, g_ref, k_ref, v_ref, o_ref):
    for quad in range(N_QUADS):
        lanes = slice(quad * QUAD_W, (quad + 1) * QUAD_W)
        out = _quad_attention(q_ref[0, :, lanes], None, None, k_ref[0, :, lanes], v_ref[0, :, lanes], None)
        o_ref[0, :, lanes] = (out * g_ref[0, :, lanes].astype(jnp.float32)).astype(jnp.bfloat16)


def _cattn(pc):
    b, lc, _ = pc.shape
    spec = lambda j: pl.BlockSpec((1, lc, GROUP_W), lambda i: (i, 0, j))
    return pl.pallas_call(
        _cattn_kernel,
        grid=(b,),
        in_specs=[spec(2), spec(5), spec(3), spec(4)],
        out_specs=pl.BlockSpec((1, lc, NA_WIDTH), lambda i: (i, 0, 0)),
        out_shape=jax.ShapeDtypeStruct((b, lc, NA_WIDTH), jnp.bfloat16),
        compiler_params=_cparams("parallel"),
        name="cattn",
    )(pc, pc, pc, pc)


def _bias_tables(rel_bias):
    col = np.arange(GRID_W)
    cs = np.clip(col - WIN_COLS // 2, 0, GRID_W - WIN_COLS)
    in_win = (col[None, :] >= cs[:, None]) & (col[None, :] < cs[:, None] + WIN_COLS)
    dc_idx = np.clip(col[None, :] - col[:, None], -(WIN_COLS - 1), WIN_COLS - 1) + (WIN_COLS - 1)
    onehot = (np.arange(2 * WIN_COLS - 1)[:, None, None] == dc_idx[None]).astype(np.float32)
    by_dr = jnp.einsum("lhrc,cqk->lhrqk", rel_bias, jnp.asarray(onehot), precision=jax.lax.Precision.HIGHEST)
    by_dr = jnp.where(in_win[None, None, None], by_dr * LOG2E, NEG_INF)
    return jnp.concatenate([by_dr[:, :, :-1], by_dr[:, :, 1:]], axis=-1)


def kernel(x, c, ctx, c_ctx, norm_g, w_ada, b_ada, w_in, w_four, q_norm_g, k_norm_g, rel_bias, w_out):
    batch, seq, d = x.shape
    bf16 = jnp.bfloat16

    cond = jnp.zeros((MOD_ROWS, d), jnp.float32).at[:batch].set(c).at[batch].set(c_ctx)
    mods = _modulation(cond, w_ada, b_ada)

    params = {
        "mods": mods.reshape(DEPTH, MOD_ROWS, 1, 3 * d),
        "norm_g": norm_g.reshape(DEPTH, 1, d),
        "w_in": w_in.astype(bf16),
        "w_out": w_out.astype(bf16),
        "qg": (jnp.tile(q_norm_g, (1, NA_HEADS)) * (HEAD_DIM ** -0.5 * LOG2E)).reshape(DEPTH, 1, NA_WIDTH),
        "kg": jnp.tile(k_norm_g, (1, NA_HEADS)).reshape(DEPTH, 1, NA_WIDTH),
        "head_mean": jnp.asarray(
            np.kron(np.eye(HEADS_PER_QUAD), np.full((HEAD_DIM, HEAD_DIM), 1.0 / HEAD_DIM)), jnp.float32).astype(bf16),
    }
    w_four_b = w_four.astype(bf16)
    bias = _bias_tables(rel_bias)
    fft_tabs = _fft_tables(seq)
    chan_c, seq_c = _dft_tables(ctx.shape[1])

    x_mod = lambda i: i
    c_mod = lambda i: batch

    (px,) = _stream(x, x_mod, params, nxt=0)
    (pc,) = _stream(ctx, c_mod, params, nxt=0)
    for l in range(DEPTH):
        fo_x = _fourier_fft(px, fft_tabs, w_four_b, l)
        na_x = _nattn(px, pc, bias, l)
        if l + 1 < DEPTH:
            fo_c = _fourier(pc, chan_c, seq_c, w_four_b, l)
            na_c = _cattn(pc)
            x, px = _stream(x, x_mod, params, prev=(l, fo_x, na_x), nxt=l + 1)
            ctx, pc = _stream(ctx, c_mod, params, prev=(l, fo_c, na_c), nxt=l + 1)
        else:
            (x,) = _stream(x, x_mod, params, prev=(l, fo_x, na_x))
    return x
```

```python
import functools

import numpy as np
import jax
import jax.numpy as jnp
from jax.experimental import pallas as pl
from jax.experimental.pallas import tpu as pltpu

D_MODEL = 1024
DEPTH = 4
GRID_W = 64
F_WIDTH = 512
F_GROUPS = 4
F_GROUP_DIM = F_WIDTH // F_GROUPS
HEAD_DIM = 64
NA_WIDTH = 512
NA_HEADS = NA_WIDTH // HEAD_DIM
IN_COLS = 2 * F_WIDTH + 4 * NA_WIDTH
WIN_ROWS = 8
WIN_COLS = 16
EPS = 1e-6
NEG_INF = -1e30
LOG2E = 1.4426950408889634

GROUP_W = 512
N_GROUPS = IN_COLS // GROUP_W
QUAD_W = 256
HEADS_PER_QUAD = QUAD_W // HEAD_DIM
N_QUADS = NA_WIDTH // QUAD_W
LANES = 128
MOD_ROWS = 16
STREAM_ROWS = 1024
STREAM_SUB_ROWS = 256
CTX_ENTRIES_PER_STEP = 2

VMEM_LIMIT = 56 * 1024 * 1024


def _cparams(*sem, flags=None):
    return pltpu.CompilerParams(dimension_semantics=sem, vmem_limit_bytes=VMEM_LIMIT, flags=flags)


def _dot(a, b):
    return jnp.dot(a, b, preferred_element_type=jnp.float32)


def _dot_nt(a, b):
    return jax.lax.dot_general(a, b, (((1,), (1,)), ((), ())), preferred_element_type=jnp.float32)


def _silu(t):
    return t * (1.0 / (1.0 + jnp.exp(-t)))


def _mod_kernel(cond_ref, w_ref, b_ref, o_ref):
    cond = cond_ref[...]
    o_ref[0] = _dot(_silu(cond), w_ref[0]) + b_ref[0]


def _modulation(cond, w_ada, b_ada):
    tn = 1024
    return pl.pallas_call(
        _mod_kernel,
        grid=(DEPTH, 3 * D_MODEL // tn),
        in_specs=[
            pl.BlockSpec((MOD_ROWS, D_MODEL), lambda l, j: (0, 0)),
            pl.BlockSpec((1, D_MODEL, tn), lambda l, j: (l, 0, j)),
            pl.BlockSpec((1, 1, tn), lambda l, j: (l, 0, j)),
        ],
        out_specs=pl.BlockSpec((1, MOD_ROWS, tn), lambda l, j: (l, 0, j)),
        out_shape=jax.ShapeDtypeStruct((DEPTH, MOD_ROWS, 3 * D_MODEL), jnp.float32),
        compiler_params=_cparams("parallel", "parallel"),
        name="adaln_mod",
    )(cond, w_ada, b_ada.reshape(DEPTH, 1, 3 * D_MODEL))


def _stream_kernel(has_prev, has_next, *refs):
    refs = list(refs)
    x_ref = refs.pop(0)
    if has_prev:
        fo_ref, na_ref, gate_ref, wout_ref = refs[:4]
        refs = refs[4:]
    if has_next:
        mod_ref, ng_ref, win_ref, qg_ref, kg_ref = refs[:5]
        refs = refs[5:]
    if has_prev:
        xo_ref = refs.pop(0)
    if has_next:
        p_ref = refs.pop(0)

    tm = x_ref.shape[1]
    sub = min(STREAM_SUB_ROWS, tm)
    blocks = [slice(s * sub, (s + 1) * sub) for s in range(tm // sub)]
    xs = [x_ref[0, rows, :] for rows in blocks]
    if has_prev:
        for s, rows in enumerate(blocks):
            mix = (_dot(fo_ref[0, rows, :], wout_ref[:F_WIDTH, :].astype(jnp.bfloat16))
                   + _dot(na_ref[0, rows, :], wout_ref[F_WIDTH:, :].astype(jnp.bfloat16)))
            xs[s] = xs[s] + gate_ref[0] * mix
            xo_ref[0, rows, :] = xs[s]
    if has_next:
        shift = mod_ref[0, :, :D_MODEL]
        scale = mod_ref[0, :, D_MODEL:2 * D_MODEL]
        low_head = jax.lax.broadcasted_iota(jnp.int32, (sub, LANES), 1) < HEAD_DIM
        for x, rows in zip(xs, blocks):
            ms = jnp.mean(x * x, axis=-1, keepdims=True)
            y = x * jax.lax.rsqrt(ms + EPS) * ng_ref[...]
            y = (y * (1.0 + scale) + shift).astype(jnp.bfloat16)
            for j in range(N_GROUPS):
                pj = _dot(y, win_ref[:, j * GROUP_W:(j + 1) * GROUP_W].astype(jnp.bfloat16))
                if j in (1, 5):
                    pj = _silu(pj)
                elif j in (2, 3):
                    sq = pj * pj
                    tiles = []
                    for lt in range(GROUP_W // LANES):
                        sq_t = sq[:, lt * LANES:(lt + 1) * LANES]
                        lo = jnp.sum(jnp.where(low_head, sq_t, 0.0), axis=-1, keepdims=True)
                        hi = jnp.sum(jnp.where(low_head, 0.0, sq_t), axis=-1, keepdims=True)
                        tiles.append(jnp.where(low_head, lo, hi))
                    ms_h = jnp.concatenate(tiles, axis=1) * (1.0 / HEAD_DIM)
                    g = qg_ref[...] if j == 2 else kg_ref[...]
                    pj = pj * jax.lax.rsqrt(ms_h + EPS) * g
                p_ref[0, rows, j * GROUP_W:(j + 1) * GROUP_W] = pj.astype(jnp.bfloat16)


def _stream(x, mod_index, params, prev=None, nxt=None):
    b, l, d = x.shape
    tm = min(STREAM_ROWS, l)
    once = pl.Buffered(1)
    in_specs = [pl.BlockSpec((1, tm, d), lambda i, t: (i, t, 0))]
    args = [x]
    out_specs, out_shape = [], []
    if prev is not None:
        lp, f_out, na_out = prev
        in_specs += [
            pl.BlockSpec((1, tm, F_WIDTH), lambda i, t: (i, t, 0)),
            pl.BlockSpec((1, tm, NA_WIDTH), lambda i, t: (i, t, 0)),
            pl.BlockSpec((None, 1, 1, d), lambda i, t: (lp, mod_index(i), 0, 2)),
            pl.BlockSpec((None, d, d), lambda i, t: (lp, 0, 0), pipeline_mode=once),
        ]
        args += [f_out, na_out, params["mods"], params["w_out"]]
        out_specs.append(pl.BlockSpec((1, tm, d), lambda i, t: (i, t, 0)))
        out_shape.append(jax.ShapeDtypeStruct((b, l, d), jnp.float32))
    if nxt is not None:
        ln = nxt
        in_specs += [
            pl.BlockSpec((None, 1, 1, 3 * d), lambda i, t: (ln, mod_index(i), 0, 0)),
            pl.BlockSpec((None, 1, d), lambda i, t: (ln, 0, 0)),
            pl.BlockSpec((None, d, IN_COLS), lambda i, t: (ln, 0, 0), pipeline_mode=once),
            pl.BlockSpec((None, 1, GROUP_W), lambda i, t: (ln, 0, 0)),
            pl.BlockSpec((None, 1, GROUP_W), lambda i, t: (ln, 0, 0)),
        ]
        args += [params["mods"], params["norm_g"], params["w_in"], params["qg"], params["kg"]]
        out_specs.append(pl.BlockSpec((1, tm, IN_COLS), lambda i, t: (i, t, 0)))
        out_shape.append(jax.ShapeDtypeStruct((b, l, IN_COLS), jnp.bfloat16))
    outs = pl.pallas_call(
        functools.partial(_stream_kernel, prev is not None, nxt is not None),
        grid=(b, l // tm),
        in_specs=in_specs,
        out_specs=out_specs,
        out_shape=out_shape,
        compiler_params=_cparams("parallel", "parallel"),
        name="stream_" + ("o" if prev is not None else "") + ("i" if nxt is not None else ""),
    )(*args)
    return outs


def _ctx_mixers_kernel(n_entries, p_ref, cs_ref, dl_ref, wf_ref, fo_ref, na_ref):
    bf16, f32 = jnp.bfloat16, jnp.float32
    lc = p_ref.shape[0] // n_entries
    for e in range(n_entries):
        rows = slice(e * lc, (e + 1) * lc)
        group = lambda j: p_ref[rows, j * GROUP_W:(j + 1) * GROUP_W]
        uc = _dot(group(0), cs_ref[...])
        ab = jnp.concatenate([uc[:, :F_WIDTH], uc[:, F_WIDTH:]], axis=0).astype(bf16)
        y = _dot(dl_ref[...], ab).astype(bf16)
        fo_ref[rows, :] = (_dot(y, wf_ref[...]) * group(1).astype(f32)).astype(bf16)
        q, k, v, gate = group(2), group(3), group(4), group(5)
        for quad in range(N_QUADS):
            lanes = slice(quad * QUAD_W, (quad + 1) * QUAD_W)
            out = _quad_attention_ctx(q[:, lanes], k[:, lanes], v[:, lanes])
            na_ref[rows, lanes] = (out * gate[:, lanes].astype(f32)).astype(bf16)


def _ctx_mixers(pc_flat, n_entries, chan_tab, seq_tab, w_four, layer):
    _, rows, _ = pc_flat.shape
    lc = rows // n_entries
    per_step = CTX_ENTRIES_PER_STEP
    out = pl.BlockSpec((None, per_step * lc, GROUP_W), lambda i: (0, i, 0))
    return pl.pallas_call(
        functools.partial(_ctx_mixers_kernel, per_step),
        grid=(n_entries // per_step,),
        in_specs=[
            pl.BlockSpec((None, per_step * lc, IN_COLS), lambda i: (0, i, 0)),
            pl.BlockSpec((F_WIDTH, 2 * F_WIDTH), lambda i: (0, 0)),
            pl.BlockSpec((lc, 2 * lc), lambda i: (0, 0)),
            pl.BlockSpec((None, F_WIDTH, F_WIDTH), lambda i: (layer, 0, 0)),
        ],
        out_specs=[out, out],
        out_shape=[jax.ShapeDtypeStruct((1, rows, GROUP_W), jnp.bfloat16)] * 2,
        compiler_params=_cparams("parallel"),
        name="ctx_mixers",
    )(pc_flat, chan_tab, seq_tab, w_four)


def _dft_tables(seq):
    n = np.arange(seq, dtype=np.int64)
    ang = 2.0 * np.pi * ((n[:, None] * n[None, :]) % seq) / seq
    seq_tab = np.concatenate([np.cos(ang), -np.sin(ang)], axis=1)
    m = np.arange(F_GROUP_DIM, dtype=np.int64)
    angc = 2.0 * np.pi * ((m[:, None] * m[None, :]) % F_GROUP_DIM) / F_GROUP_DIM
    eye = np.eye(F_GROUPS)
    scale = (seq * F_GROUP_DIM) ** -0.5
    chan_tab = np.concatenate([np.kron(eye, np.cos(angc)), np.kron(eye, np.sin(angc))], axis=1) * scale
    return (jnp.asarray(chan_tab, jnp.float32).astype(jnp.bfloat16),
            jnp.asarray(seq_tab, jnp.float32).astype(jnp.bfloat16))


def _complex_dft8(xr, xi):
    c = 0.5 ** 0.5
    sr = [xr[n] + xr[n + 4] for n in range(4)]
    si = [xi[n] + xi[n + 4] for n in range(4)]
    dr = [xr[n] - xr[n + 4] for n in range(4)]
    di = [xi[n] - xi[n + 4] for n in range(4)]
    t0r, t0i, t1r, t1i = sr[0] + sr[2], si[0] + si[2], sr[0] - sr[2], si[0] - si[2]
    t2r, t2i, t3r, t3i = sr[1] + sr[3], si[1] + si[3], sr[1] - sr[3], si[1] - si[3]
    e1r, e1i = c * (dr[1] + di[1]), c * (di[1] - dr[1])
    e2r, e2i = di[2], -dr[2]
    e3r, e3i = c * (di[3] - dr[3]), -c * (dr[3] + di[3])
    u0r, u0i, u1r, u1i = dr[0] + e2r, di[0] + e2i, dr[0] - e2r, di[0] - e2i
    u2r, u2i, u3r, u3i = e1r + e3r, e1i + e3i, e1r - e3r, e1i - e3i
    vr = [t0r + t2r, u0r + u2r, t1r + t3i, u1r + u3i, t0r - t2r, u0r - u2r, t1r - t3i, u1r - u3i]
    vi = [t0i + t2i, u0i + u2i, t1i - t3r, u1i - u3r, t0i - t2i, u0i - u2i, t1i + t3r, u1i + u3r]
    return vr, vi


FFT_N1 = 8
FFT_N2 = 256
FFT_DECIM = FFT_N2 // FFT_N1
FFT_BFLY_ROWS = 32


def _fold_kernel(cs_ref, wf_ref, o_ref):
    o_ref[0] = _dot(cs_ref[...], wf_ref[0]).astype(jnp.bfloat16)


def _fold_channel_dft(chan_tab, w_four):
    return pl.pallas_call(
        _fold_kernel,
        grid=(DEPTH,),
        in_specs=[pl.BlockSpec((2 * F_WIDTH, F_WIDTH), lambda l: (0, 0)),
                  pl.BlockSpec((1, F_WIDTH, F_WIDTH), lambda l: (l, 0, 0))],
        out_specs=pl.BlockSpec((1, 2 * F_WIDTH, F_WIDTH), lambda l: (l, 0, 0)),
        out_shape=jax.ShapeDtypeStruct((DEPTH, 2 * F_WIDTH, F_WIDTH), jnp.bfloat16),
        compiler_params=_cparams("parallel"),
        name="fold_channel_dft",
    )(chan_tab, w_four)


def _fourier_fft_kernel(u_ref, gate_ref, perm_ref, dft_ref, twc_ref, tws_ref, w2_ref, o_ref,
                        up_ref, br_ref, bi_ref):
    bf16 = jnp.bfloat16
    n_nat = u_ref.shape[1] // FFT_N2
    for j in range(n_nat):
        t = _dot(perm_ref[...], u_ref[0, j * FFT_N2:(j + 1) * FFT_N2, :]).astype(bf16)
        for n1 in range(FFT_N1):
            up_ref[n1, j * FFT_DECIM:(j + 1) * FFT_DECIM, :] = t[n1 * FFT_DECIM:(n1 + 1) * FFT_DECIM]
    for n1 in range(FFT_N1):
        b = _dot(dft_ref[...], up_ref[n1])
        br, bi = b[:FFT_N2], b[FFT_N2:]
        if n1 > 0:
            tc = jnp.concatenate([twc_ref[n1]] * (F_WIDTH // LANES), axis=1)
            ts = jnp.concatenate([tws_ref[n1]] * (F_WIDTH // LANES), axis=1)
            br, bi = br * tc + bi * ts, bi * tc - br * ts
        br_ref[n1] = br
        bi_ref[n1] = bi
    for ch in range(FFT_N2 // FFT_BFLY_ROWS):
        rows = slice(ch * FFT_BFLY_ROWS, (ch + 1) * FFT_BFLY_ROWS)
        vr, vi = _complex_dft8([br_ref[n, rows, :] for n in range(FFT_N1)],
                               [bi_ref[n, rows, :] for n in range(FFT_N1)])
        spec = jnp.concatenate([jnp.concatenate([vr[k1].astype(bf16), vi[k1].astype(bf16)], axis=1)
                                for k1 in range(FFT_N1)], axis=0)
        y = _dot(spec, w2_ref[...])
        for k1 in range(FFT_N1):
            out_rows = slice(k1 * FFT_N2 + ch * FFT_BFLY_ROWS, k1 * FFT_N2 + (ch + 1) * FFT_BFLY_ROWS)
            gate = gate_ref[0, out_rows, :].astype(jnp.float32)
            o_ref[0, out_rows, :] = (y[k1 * FFT_BFLY_ROWS:(k1 + 1) * FFT_BFLY_ROWS] * gate).astype(bf16)


def _fourier_fft(p, tabs, w2, layer):
    b, seq, _ = p.shape
    assert seq == FFT_N1 * FFT_N2
    const = lambda *shape: pl.BlockSpec(shape, lambda i: (0,) * len(shape))
    return pl.pallas_call(
        _fourier_fft_kernel,
        grid=(b,),
        in_specs=[
            pl.BlockSpec((1, seq, GROUP_W), lambda i: (i, 0, 0)),
            pl.BlockSpec((1, seq, GROUP_W), lambda i: (i, 0, 1)),
            const(FFT_N2, FFT_N2),
            const(2 * FFT_N2, FFT_N2),
            const(FFT_N1, FFT_N2, LANES),
            const(FFT_N1, FFT_N2, LANES),
            pl.BlockSpec((None, 2 * F_WIDTH, F_WIDTH), lambda i: (layer, 0, 0)),
        ],
        out_specs=pl.BlockSpec((1, seq, F_WIDTH), lambda i: (i, 0, 0)),
        out_shape=jax.ShapeDtypeStruct((b, seq, F_WIDTH), jnp.bfloat16),
        scratch_shapes=[
            pltpu.VMEM((FFT_N1, FFT_N2, F_WIDTH), jnp.bfloat16),
            pltpu.VMEM((FFT_N1, FFT_N2, F_WIDTH), jnp.float32),
            pltpu.VMEM((FFT_N1, FFT_N2, F_WIDTH), jnp.float32),
        ],
        compiler_params=_cparams("parallel"),
        name="fourier_fft",
    )(p, p, *tabs, w2)


def _fft_tables(seq):
    assert seq == FFT_N1 * FFT_N2
    rows = np.arange(FFT_N2)
    perm = np.zeros((FFT_N2, FFT_N2))
    for n1 in range(FFT_N1):
        for m in range(FFT_DECIM):
            perm[n1 * FFT_DECIM + m, FFT_N1 * m + n1] = 1.0
    ang = 2.0 * np.pi * ((rows[:, None] * rows[None, :]) % FFT_N2) / FFT_N2
    dft = np.concatenate([np.cos(ang), -np.sin(ang)], axis=0)
    angt = 2.0 * np.pi * (np.arange(FFT_N1)[:, None] * rows[None, :]) / seq
    twc = np.broadcast_to(np.cos(angt)[:, :, None], (FFT_N1, FFT_N2, LANES))
    tws = np.broadcast_to(np.sin(angt)[:, :, None], (FFT_N1, FFT_N2, LANES))
    m = np.arange(F_GROUP_DIM, dtype=np.int64)
    angc = 2.0 * np.pi * ((m[:, None] * m[None, :]) % F_GROUP_DIM) / F_GROUP_DIM
    eye = np.eye(F_GROUPS)
    scale = (seq * F_GROUP_DIM) ** -0.5
    chan = np.concatenate([np.kron(eye, np.cos(angc)), np.kron(eye, np.sin(angc))], axis=0) * scale
    f32 = lambda a: jnp.asarray(np.ascontiguousarray(a), jnp.float32)
    return (f32(perm).astype(jnp.bfloat16), f32(dft).astype(jnp.bfloat16), f32(twc), f32(tws)), \
        f32(chan).astype(jnp.bfloat16)


COL_BLOCK = WIN_COLS
N_COL_BLOCKS = GRID_W // COL_BLOCK
BLOCK_KEYS = WIN_ROWS * COL_BLOCK
LIVE_BLOCKS = 2
LIVE_W = LIVE_BLOCKS * BLOCK_KEYS


def _live_segments():
    col = np.arange(GRID_W)
    cs = np.clip(col - WIN_COLS // 2, 0, GRID_W - WIN_COLS)
    lo = np.minimum(cs // COL_BLOCK, N_COL_BLOCKS - LIVE_BLOCKS)
    segs = []
    for g in range(0, GRID_W, 8):
        kb = int(lo[g:g + 8].min())
        assert np.all((cs[g:g + 8] + WIN_COLS - 1) // COL_BLOCK <= kb + LIVE_BLOCKS - 1)
        if segs and segs[-1][2] == kb:
            segs[-1] = (segs[-1][0], g + 8, kb)
        else:
            segs.append((g, g + 8, kb))
    return segs


LIVE_SEGMENTS = _live_segments()


def _stack_heads(qq):
    n = qq.shape[0]
    head_of_lane = jax.lax.broadcasted_iota(jnp.int32, (n, QUAD_W), 1) // HEAD_DIM
    zero = jnp.zeros_like(qq)
    lhs = jnp.concatenate([jnp.where(head_of_lane == h, qq, zero) for h in range(HEADS_PER_QUAD)], axis=0)
    return lhs, head_of_lane


def _unstack_heads(o, head_of_lane):
    n = head_of_lane.shape[0]
    out = o[:n]
    for h in range(1, HEADS_PER_QUAD):
        out = jnp.where(head_of_lane == h, o[h * n:(h + 1) * n], out)
    return out


def _quad_attention_ctx(qq, k_ctx, v_ctx):
    lhs, head_of_lane = _stack_heads(qq)
    s = _dot_nt(lhs, k_ctx)
    e = jnp.exp2(s - jnp.max(s, axis=-1, keepdims=True))
    o = _dot(e.astype(jnp.bfloat16), v_ctx) * (1.0 / jnp.sum(e, axis=-1, keepdims=True))
    return _unstack_heads(o, head_of_lane)


def _quad_attention_local(qq, k_loc, v_loc, k_ctx, v_ctx, bias):
    n = qq.shape[0]
    lhs, head_of_lane = _stack_heads(qq)
    s_ctx = _dot_nt(lhs, k_ctx)
    s_loc = _dot_nt(lhs, k_loc)
    live = jnp.concatenate([s_loc[h * n + a:h * n + b, kb * BLOCK_KEYS:kb * BLOCK_KEYS + LIVE_W]
                            for h in range(HEADS_PER_QUAD) for a, b, kb in LIVE_SEGMENTS], axis=0) + bias
    m = jnp.maximum(jnp.max(s_ctx, axis=-1, keepdims=True), jnp.max(live, axis=-1, keepdims=True))
    e_ctx = jnp.exp2(s_ctx - m)
    e_live = jnp.exp2(live - m)
    denom = jnp.sum(e_ctx, axis=-1, keepdims=True) + jnp.sum(e_live, axis=-1, keepdims=True)
    n_keys = k_loc.shape[0]
    rows = []
    for h in range(HEADS_PER_QUAD):
        for a, b, kb in LIVE_SEGMENTS:
            parts = []
            if kb > 0:
                parts.append(jnp.zeros((b - a, kb * BLOCK_KEYS), jnp.float32))
            parts.append(e_live[h * n + a:h * n + b])
            if kb * BLOCK_KEYS + LIVE_W < n_keys:
                parts.append(jnp.zeros((b - a, n_keys - LIVE_W - kb * BLOCK_KEYS), jnp.float32))
            rows.append(jnp.concatenate(parts, axis=1))
    p_loc = jnp.concatenate(rows, axis=0).astype(jnp.bfloat16)
    o = _dot(e_ctx.astype(jnp.bfloat16), v_ctx) + _dot(p_loc, v_loc)
    return _unstack_heads(o * (1.0 / denom), head_of_lane)


def _nattn_kernel(q_ref, g_ref, k_ref, v_ref, kc_ref, vc_ref, bias_ref, o_ref):
    n_rows = k_ref.shape[1] // GRID_W
    half = WIN_ROWS // 2

    def band(ref, rs, lanes):
        return jnp.concatenate(
            [ref[0, pl.ds((rs + j) * GRID_W + kb * COL_BLOCK, COL_BLOCK), lanes]
             for kb in range(N_COL_BLOCKS) for j in range(WIN_ROWS)], axis=0)

    for r in range(n_rows):
        rs = min(max(r - half, 0), n_rows - WIN_ROWS)
        first = rs - r + (WIN_ROWS - 1)
        qrows = pl.ds(r * GRID_W, GRID_W)
        for quad in range(N_QUADS):
            lanes = slice(quad * QUAD_W, (quad + 1) * QUAD_W)
            bias = jnp.concatenate([bias_ref[first, quad * HEADS_PER_QUAD + h] for h in range(HEADS_PER_QUAD)],
                                   axis=0)
            out = _quad_attention_local(q_ref[0, qrows, lanes], band(k_ref, rs, lanes), band(v_ref, rs, lanes),
                                        kc_ref[0, :, lanes], vc_ref[0, :, lanes], bias)
            o_ref[0, qrows, lanes] = (out * g_ref[0, qrows, lanes].astype(jnp.float32)).astype(jnp.bfloat16)


def _nattn(px, pc, bias, layer):
    b, seq, _ = px.shape
    lc = pc.shape[1]
    group = lambda rows, j: pl.BlockSpec((1, rows, GROUP_W), lambda i: (i, 0, j))
    return pl.pallas_call(
        _nattn_kernel,
        grid=(b,),
        in_specs=[
            group(seq, 2), group(seq, 5), group(seq, 3), group(seq, 4), group(lc, 3), group(lc, 4),
            pl.BlockSpec((None, WIN_ROWS, NA_HEADS, GRID_W, LIVE_W), lambda i: (layer, 0, 0, 0, 0)),
        ],
        out_specs=pl.BlockSpec((1, seq, NA_WIDTH), lambda i: (i, 0, 0)),
        out_shape=jax.ShapeDtypeStruct((b, seq, NA_WIDTH), jnp.bfloat16),
        compiler_params=_cparams("parallel"),
        name="nattn",
    )(px, px, px, px, pc, pc, bias)


def _bias_tables(rel_bias):
    col = np.arange(GRID_W)
    cs = np.clip(col - WIN_COLS // 2, 0, GRID_W - WIN_COLS)
    in_win = (col[None, :] >= cs[:, None]) & (col[None, :] < cs[:, None] + WIN_COLS)
    dc_idx = np.clip(col[None, :] - col[:, None], -(WIN_COLS - 1), WIN_COLS - 1) + (WIN_COLS - 1)
    n_dc = 2 * WIN_COLS - 1
    onehot = np.zeros((WIN_ROWS * n_dc + 1, GRID_W, LIVE_BLOCKS, WIN_ROWS, COL_BLOCK), np.float32)
    for a, b, kb in LIVE_SEGMENTS:
        for c in range(a, b):
            for w in range(LIVE_BLOCKS):
                for k in range(COL_BLOCK):
                    kcol = (kb + w) * COL_BLOCK + k
                    for j in range(WIN_ROWS):
                        if in_win[c, kcol]:
                            onehot[j * n_dc + dc_idx[c, kcol], c, w, j, k] = 1.0
                        else:
                            onehot[WIN_ROWS * n_dc, c, w, j, k] = NEG_INF
    onehot = onehot.reshape(WIN_ROWS * n_dc + 1, GRID_W, LIVE_W)
    depth, heads = rel_bias.shape[:2]
    by_first = jnp.stack([rel_bias[:, :, f:f + WIN_ROWS] for f in range(WIN_ROWS)], axis=1) * LOG2E
    by_first = by_first.reshape(depth, WIN_ROWS, heads, WIN_ROWS * n_dc)
    by_first = jnp.concatenate([by_first, jnp.ones((depth, WIN_ROWS, heads, 1), rel_bias.dtype)], axis=-1)
    tab = jnp.einsum("lfhk,kcn->lfchn", by_first.astype(jnp.bfloat16), jnp.asarray(onehot).astype(jnp.bfloat16),
                     preferred_element_type=jnp.float32)
    return jnp.transpose(tab, (0, 1, 3, 2, 4))


def kernel(x, c, ctx, c_ctx, norm_g, w_ada, b_ada, w_in, w_four, q_norm_g, k_norm_g, rel_bias, w_out):
    batch, seq, d = x.shape
    bf16 = jnp.bfloat16

    cond = jnp.zeros((MOD_ROWS, d), jnp.float32).at[:batch].set(c).at[batch].set(c_ctx)
    mods = _modulation(cond, w_ada, b_ada)

    params = {
        "mods": mods.reshape(DEPTH, MOD_ROWS, 1, 3 * d),
        "norm_g": norm_g.reshape(DEPTH, 1, d),
        "w_in": w_in,
        "w_out": w_out,
        "qg": (jnp.tile(q_norm_g, (1, NA_HEADS)) * (HEAD_DIM ** -0.5 * LOG2E)).reshape(DEPTH, 1, NA_WIDTH),
        "kg": jnp.tile(k_norm_g, (1, NA_HEADS)).reshape(DEPTH, 1, NA_WIDTH),
    }
    w_four_b = w_four.astype(bf16)
    bias = _bias_tables(rel_bias)
    fft_tabs, chan_x = _fft_tables(seq)
    w2 = _fold_channel_dft(chan_x, w_four_b)
    chan_c, seq_c = _dft_tables(ctx.shape[1])

    x_mod = lambda i: i
    c_mod = lambda i: batch

    lc = ctx.shape[1]
    ctx = ctx.reshape(1, batch * lc, d)

    (px,) = _stream(x, x_mod, params, nxt=0)
    (pc,) = _stream(ctx, c_mod, params, nxt=0)
    for l in range(DEPTH):
        fo_x = _fourier_fft(px, fft_tabs, w2, l)
        na_x = _nattn(px, pc.reshape(batch, lc, IN_COLS), bias, l)
        if l + 1 < DEPTH:
            fo_c, na_c = _ctx_mixers(pc, batch, chan_c, seq_c, w_four_b, l)
            x, px = _stream(x, x_mod, params, prev=(l, fo_x, na_x), nxt=l + 1)
            ctx, pc = _stream(ctx, c_mod, params, prev=(l, fo_c, na_c), nxt=l + 1)
        else:
            (x,) = _stream(x, x_mod, params, prev=(l, fo_x, na_x))
    return x
```
